```python
import math
import jax, jax.numpy as jnp
from jax import lax
import numpy as np

D_MODEL = 2048
BATCH = 4
SEQ = 4096
DEPTH = 4

MIX_W = D_MODEL
GROUP_W = MIX_W // 4
HEAD_DIM = 64
CONV_A_CH = GROUP_W
CONV_A_K = 3
SWA_HEADS = GROUP_W // HEAD_DIM
SWA_KV_HEADS = 2
SWA_WINDOW = 128
SSM_D_INNER = GROUP_W
SSM_HEADDIM = 64
SSM_HEADS = SSM_D_INNER // SSM_HEADDIM
SSM_GROUPS = 2
SSM_STATE = 128
SSM_CONV_K = 4
SSM_CHUNK = 128
SSM_XBC = SSM_D_INNER + 2 * SSM_GROUPS * SSM_STATE
SB_HEADS = GROUP_W // HEAD_DIM
Q_BLOCK = 128
N_EXPERTS = 16
N_EXPERT_GROUPS = 4
EXPERTS_PER_GROUP = N_EXPERTS // N_EXPERT_GROUPS
TOPK_GROUPS = 1
TOP_K = 2
D_FF = 1024
MOE_BLOCK = 128
N_MOD = 6
EPS = 1e-6

IN_A = 3 * CONV_A_CH
IN_B = SWA_HEADS * HEAD_DIM + 2 * SWA_KV_HEADS * HEAD_DIM
IN_C = SSM_D_INNER + SSM_XBC + SSM_HEADS
IN_D = 3 * SB_HEADS * HEAD_DIM
IN_DIM = IN_A + IN_B + IN_C + IN_D

kernel_name = "hybrid_parallel_groups_adaln_moe"


def rmsnorm(x, g):
    xf = x.astype(jnp.float32)
    y = xf * lax.rsqrt(jnp.mean(xf * xf, axis=-1, keepdims=True) + EPS)
    return (y * g.astype(jnp.float32)).astype(x.dtype)


def adaln(x, g, shift, scale):
    return rmsnorm(x, g) * (1 + scale) + shift


def causal_depthwise_conv(u, w):
    k = w.shape[0]
    ch = u.shape[-1]
    return lax.conv_general_dilated(
        u, w[:, None, :].astype(u.dtype), window_strides=(1,), padding=((k - 1, 0),),
        dimension_numbers=('NWC', 'WIO', 'NWC'), feature_group_count=ch)


def alibi_slopes(n):
    return 2.0 ** (-8.0 * jnp.arange(1, n + 1, dtype=jnp.float32) / n)


def short_conv_mixer(p, conv_w):
    b_gate, c_gate, xh = jnp.split(p, 3, axis=-1)
    return b_gate * causal_depthwise_conv(c_gate * xh, conv_w)


def swa_sink_attention(q, k, v, sinks):
    b, s, hq, d = q.shape
    hkv = k.shape[2]
    g = hq // hkv
    w = Q_BLOCK
    nb = s // w
    qb = q.reshape(b, nb, w, hkv, g, d)

    def with_prev(t):
        tb = t.reshape(b, nb, w, hkv, d)
        prev = jnp.pad(tb, ((0, 0), (1, 0), (0, 0), (0, 0), (0, 0)))[:, :-1]
        return jnp.concatenate([prev, tb], axis=2)

    kb, vb = with_prev(k), with_prev(v)
    scores = jnp.einsum('bnqhgd,bnkhd->bnhgqk', qb, kb).astype(jnp.float32) / math.sqrt(d)
    qi = jnp.arange(w)[:, None]
    kj = jnp.arange(2 * w)[None, :]
    dist = qi + w - kj
    key_abs = jnp.arange(nb)[:, None] * w + jnp.arange(2 * w)[None, :] - w
    valid = ((dist >= 0) & (dist < SWA_WINDOW))[None] & (key_abs >= 0)[:, None, :]
    slopes = alibi_slopes(hq).reshape(hkv, g)
    scores = scores - slopes[:, :, None, None] * dist.astype(jnp.float32)
    scores = jnp.where(valid[None, :, None, None], scores, -jnp.inf)
    sink = jnp.broadcast_to(sinks.astype(jnp.float32).reshape(hkv, g)[None, None, :, :, None, None],
                            scores.shape[:-1] + (1,))
    probs = jax.nn.softmax(jnp.concatenate([scores, sink], axis=-1), axis=-1)[..., :-1]
    out = jnp.einsum('bnhgqk,bnkhd->bnqhgd', probs.astype(v.dtype), vb)
    return out.reshape(b, s, hq * d)


def mamba2_ssd_mixer(p, conv_w, conv_b, dt_bias, a_log, d_skip, norm_g):
    f32 = jnp.float32
    b, s, _ = p.shape
    z, xbc, dt_raw = jnp.split(p, [SSM_D_INNER, SSM_D_INNER + SSM_XBC], axis=-1)
    xbc = jax.nn.silu(causal_depthwise_conv(xbc, conv_w) + conv_b)
    xs, bm, cm = jnp.split(xbc, [SSM_D_INNER, SSM_D_INNER + SSM_GROUPS * SSM_STATE], axis=-1)
    h, pd, g, n, l = SSM_HEADS, SSM_HEADDIM, SSM_GROUPS, SSM_STATE, SSM_CHUNK
    nc = s // l
    rep = h // g
    xs = xs.astype(f32).reshape(b, nc, l, h, pd)
    bm = jnp.repeat(bm.astype(f32).reshape(b, nc, l, g, n), rep, axis=3)
    cm = jnp.repeat(cm.astype(f32).reshape(b, nc, l, g, n), rep, axis=3)
    dt = jax.nn.softplus(dt_raw.astype(f32) + dt_bias.astype(f32)).reshape(b, nc, l, h)
    a = -jnp.exp(a_log.astype(f32))
    da_cum = jnp.cumsum(dt * a, axis=2)
    seg = da_cum[:, :, :, None, :] - da_cum[:, :, None, :, :]
    causal = jnp.tril(jnp.ones((l, l), dtype=bool))
    decay = jnp.exp(jnp.where(causal[None, None, :, :, None], seg, -jnp.inf))
    xdt = xs * dt[..., None]
    cb = jnp.einsum('bclhn,bcshn->bclsh', cm, bm)
    y_diag = jnp.einsum('bclsh,bcshp->bclhp', cb * decay, xdt)
    decay_to_end = jnp.exp(da_cum[:, :, -1:, :] - da_cum)
    chunk_states = jnp.einsum('bclhn,bclh,bclhp->bchpn', bm, decay_to_end, xdt)
    chunk_decay = jnp.exp(da_cum[:, :, -1, :])

    def step(state, inp):
        st, dec = inp
        return state * dec[:, :, None, None] + st, state

    init = jnp.zeros((b, h, pd, n), f32)
    _, prev_states = lax.scan(step, init, (jnp.moveaxis(chunk_states, 1, 0), jnp.moveaxis(chunk_decay, 1, 0)))
    prev_states = jnp.moveaxis(prev_states, 0, 1)
    y_off = jnp.einsum('bclhn,bchpn,bclh->bclhp', cm, prev_states, jnp.exp(da_cum))
    y = y_diag + y_off + xs * d_skip.astype(f32)[:, None]
    y = y.reshape(b, s, SSM_D_INNER) * jax.nn.silu(z.astype(f32))
    return rmsnorm(y, norm_g).astype(p.dtype)


def stick_breaking_attention(q, k, v):
    b, s, h, d = q.shape
    nb = s // Q_BLOCK
    qb = jnp.moveaxis(q.reshape(b, nb, Q_BLOCK, h, d), 1, 0)
    key_pos = jnp.arange(s)

    def block(args):
        q_blk, start = args
        z = jnp.einsum('bqhd,bkhd->bhqk', q_blk, k).astype(jnp.float32) / math.sqrt(d)
        q_pos = start + jnp.arange(Q_BLOCK)
        strict = key_pos[None, :] < q_pos[:, None]
        log_keep = jnp.where(strict, -jax.nn.softplus(z), 0.0)
        after = lax.cumsum(log_keep, axis=3, reverse=True) - log_keep
        a = jnp.where(strict, jnp.exp(jax.nn.log_sigmoid(z) + after), 0.0)
        return jnp.einsum('bhqk,bkhd->bqhd', a.astype(v.dtype), v)

    out = lax.map(block, (qb, jnp.arange(nb, dtype=jnp.int32) * Q_BLOCK))
    return jnp.moveaxis(out, 0, 1).reshape(b, s, h * d)


def hybrid_mixer(h, w_in, w_out, conv_a_w, gn_a, sinks, gn_b, ssm_conv_w, ssm_conv_b,
                 dt_bias, a_log, d_skip, ssm_norm_g, gn_d):
    b, s, _ = h.shape
    proj = h @ w_in
    pa, pb, pc, pdd = jnp.split(proj, [IN_A, IN_A + IN_B, IN_A + IN_B + IN_C], axis=-1)
    ya = rmsnorm(short_conv_mixer(pa, conv_a_w), gn_a)
    nq = SWA_HEADS * HEAD_DIM
    nkv = SWA_KV_HEADS * HEAD_DIM
    qb_, kb_, vb_ = jnp.split(pb, [nq, nq + nkv], axis=-1)
    yb = swa_sink_attention(qb_.reshape(b, s, SWA_HEADS, HEAD_DIM),
                            kb_.reshape(b, s, SWA_KV_HEADS, HEAD_DIM),
                            vb_.reshape(b, s, SWA_KV_HEADS, HEAD_DIM), sinks)
    yb = rmsnorm(yb, gn_b)
    yc = mamba2_ssd_mixer(pc, ssm_conv_w, ssm_conv_b, dt_bias, a_log, d_skip, ssm_norm_g)
    qd, kd, vd = jnp.split(pdd, 3, axis=-1)
    yd = stick_breaking_attention(qd.reshape(b, s, SB_HEADS, HEAD_DIM),
                                  kd.reshape(b, s, SB_HEADS, HEAD_DIM),
                                  vd.reshape(b, s, SB_HEADS, HEAD_DIM))
    yd = rmsnorm(yd, gn_d)
    return jnp.concatenate([ya, yb, yc, yd], axis=-1) @ w_out


def grouped_moe(h, router_w, router_bias, w_gate, w_up, w_down):
    b, s, dm = h.shape
    tokens = h.reshape(-1, dm)
    t = tokens.shape[0]
    probs = jax.nn.softmax((tokens @ router_w).astype(jnp.float32), axis=-1)
    sel = probs + router_bias.astype(jnp.float32)
    grp = sel.reshape(t, N_EXPERT_GROUPS, EXPERTS_PER_GROUP)
    grp_score = lax.top_k(grp, TOP_K)[0].sum(-1)
    _, top_grp = lax.top_k(grp_score, TOPK_GROUPS)
    grp_mask = jax.nn.one_hot(top_grp, N_EXPERT_GROUPS).sum(1) > 0
    exp_mask = jnp.repeat(grp_mask, EXPERTS_PER_GROUP, axis=1)
    _, top_e = lax.top_k(jnp.where(exp_mask, sel, -jnp.inf), TOP_K)
    gate = jnp.take_along_axis(probs, top_e, axis=-1)
    gate = gate / jnp.sum(gate, axis=-1, keepdims=True)

    flat_e = top_e.reshape(-1).astype(jnp.int32)
    flat_tok = jnp.repeat(jnp.arange(t, dtype=jnp.int32), TOP_K)
    flat_w = gate.reshape(-1)
    order = jnp.argsort(flat_e)
    se, stok, sw = flat_e[order], flat_tok[order], flat_w[order]
    counts = jnp.zeros((N_EXPERTS,), jnp.int32).at[flat_e].add(1)
    starts = jnp.cumsum(counts) - counts
    padded = (counts + MOE_BLOCK - 1) // MOE_BLOCK * MOE_BLOCK
    pad_ends = jnp.cumsum(padded)
    pad_starts = pad_ends - padded
    dest = pad_starts[se] + (jnp.arange(t * TOP_K, dtype=jnp.int32) - starts[se])
    n_blocks = -(-(t * TOP_K + N_EXPERTS * MOE_BLOCK) // MOE_BLOCK)
    cap = n_blocks * MOE_BLOCK
    buf_tok = jnp.zeros((cap,), jnp.int32).at[dest].set(stok)
    buf_w = jnp.zeros((cap,), jnp.float32).at[dest].set(sw)
    xbuf = tokens[buf_tok].reshape(n_blocks, MOE_BLOCK, dm)
    blk_start = jnp.arange(n_blocks, dtype=jnp.int32) * MOE_BLOCK
    blk_e = jnp.minimum(jnp.searchsorted(pad_ends, blk_start, side='right'), N_EXPERTS - 1)

    def expert_block(args):
        xb, e = args
        hid = jax.nn.silu(xb @ w_gate[e]) * (xb @ w_up[e])
        return hid @ w_down[e]

    yb = lax.map(expert_block, (xbuf, blk_e)).reshape(cap, dm)
    yb = yb * buf_w[:, None].astype(yb.dtype)
    out = jnp.zeros_like(tokens).at[buf_tok].add(yb)
    return out.reshape(b, s, dm)


def setup_inputs(seed: int = 0) -> dict:
    key = jax.random.key(seed)
    ks = jax.random.split(key, 32)
    f32 = jnp.float32
    L, D = DEPTH, D_MODEL

    def nrm(k, shape, scale):
        return jax.random.normal(k, shape, f32) * scale

    dt0 = jnp.exp(jax.random.uniform(ks[14], (L, SSM_HEADS), f32, math.log(1e-3), math.log(1e-1)))
    return {
        "x": nrm(ks[0], (BATCH, SEQ, D), 1.0),
        "c": nrm(ks[1], (BATCH, D), 1.0),
        "w_mod": nrm(ks[2], (L, D, N_MOD * D), 0.5 * D ** -0.5),
        "b_mod": nrm(ks[3], (L, N_MOD * D), 0.01),
        "norm1_g": 1.0 + nrm(ks[4], (L, D), 0.02),
        "norm2_g": 1.0 + nrm(ks[5], (L, D), 0.02),
        "w_in": nrm(ks[6], (L, D, IN_DIM), D ** -0.5),
        "w_out": nrm(ks[7], (L, MIX_W, D), MIX_W ** -0.5),
        "conv_a_w": nrm(ks[8], (L, CONV_A_K, CONV_A_CH), CONV_A_K ** -0.5),
        "gn_a": 1.0 + nrm(ks[9], (L, GROUP_W), 0.02),
        "attn_sinks": nrm(ks[10], (L, SWA_HEADS), 1.0),
        "gn_b": 1.0 + nrm(ks[11], (L, GROUP_W), 0.02),
        "ssm_conv_w": nrm(ks[12], (L, SSM_CONV_K, SSM_XBC), SSM_CONV_K ** -0.5),
        "ssm_conv_b": nrm(ks[13], (L, SSM_XBC), 0.01),
        "dt_bias": dt0 + jnp.log(-jnp.expm1(-dt0)),
        "a_log": jnp.log(jax.random.uniform(ks[15], (L, SSM_HEADS), f32, 1.0, 16.0)),
        "d_skip": 1.0 + nrm(ks[16], (L, SSM_HEADS), 0.1),
        "ssm_norm_g": 1.0 + nrm(ks[17], (L, SSM_D_INNER), 0.02),
        "gn_d": 1.0 + nrm(ks[18], (L, GROUP_W), 0.02),
        "router_w": nrm(ks[19], (D, N_EXPERTS), D ** -0.5),
        "router_bias": nrm(ks[20], (N_EXPERTS,), 0.01),
        "moe_w_gate": nrm(ks[21], (L, N_EXPERTS, D, D_FF), D ** -0.5),
        "moe_w_up": nrm(ks[22], (L, N_EXPERTS, D, D_FF), D ** -0.5),
        "moe_w_down": nrm(ks[23], (L, N_EXPERTS, D_FF, D), D_FF ** -0.5),
        "final_g": 1.0 + nrm(ks[24], (D,), 0.02),
    }


def reference(x, c, w_mod, b_mod, norm1_g, norm2_g, w_in, w_out, conv_a_w, gn_a, attn_sinks, gn_b,
              ssm_conv_w, ssm_conv_b, dt_bias, a_log, d_skip, ssm_norm_g, gn_d, router_w, router_bias,
              moe_w_gate, moe_w_up, moe_w_down, final_g):
    b = x.shape[0]
    cond = jax.nn.silu(c)
    for l in range(DEPTH):
        mod = (cond @ w_mod[l] + b_mod[l]).reshape(b, N_MOD, 1, D_MODEL)
        h = adaln(x, norm1_g[l], mod[:, 0], mod[:, 1])
        mix = hybrid_mixer(h, w_in[l], w_out[l], conv_a_w[l], gn_a[l], attn_sinks[l], gn_b[l],
                           ssm_conv_w[l], ssm_conv_b[l], dt_bias[l], a_log[l], d_skip[l],
                           ssm_norm_g[l], gn_d[l])
        x = x + mod[:, 2] * mix
        h = adaln(x, norm2_g[l], mod[:, 3], mod[:, 4])
        x = x + mod[:, 5] * grouped_moe(h, router_w, router_bias, moe_w_gate[l], moe_w_up[l], moe_w_down[l])
    return rmsnorm(x, final_g)
```

```python
import functools
import math

import jax
import jax.numpy as jnp
from jax import lax
from jax.experimental import pallas as pl
from jax.experimental.pallas import tpu as pltpu

F32 = jnp.float32
BF16 = jnp.bfloat16

EPS = 1e-6
HEAD_DIM = 64
GROUP_W = 512
N_HEADS = GROUP_W // HEAD_DIM
SWA_KV_HEADS = 2
SWA_WINDOW = 128
SSM_GROUPS = 2
SSM_STATE = 128
SSM_CHUNK = 128
N_EXPERTS = 16
N_EXPERT_GROUPS = 4
EXPERTS_PER_GROUP = N_EXPERTS // N_EXPERT_GROUPS
N_MOD = 6

LANE = 128
HALO = 16
VMEM_LIMIT = 56 * 1024 * 1024

U_A = 0
U_DQ = 12
U_DK = 16
U_DV = 20
U_Z = 24
U_BQ = 28
U_XBC = 32
U_BK = 40
U_BV = 41
U_DT = 42
N_UNITS = 44
NP = N_UNITS * LANE

PROJ_DTYPE = jnp.float32


def _cparams(sem, vmem=VMEM_LIMIT):
    return pltpu.CompilerParams(dimension_semantics=sem, vmem_limit_bytes=vmem)


def _rms(y, g):
    return y * lax.rsqrt(jnp.mean(y * y, axis=-1, keepdims=True) + EPS) * g


def _split_bf16(a):
    hi = a.astype(BF16)
    lo = (a - hi.astype(F32)).astype(BF16)
    return hi, lo


def _dot(a, b):
    return jnp.dot(a, b, preferred_element_type=F32)


def _dot3(a, b):
    ah, al = _split_bf16(a)
    bh, bl = _split_bf16(b)
    return _dot(ah, bh) + _dot(al, bh) + _dot(ah, bl)


def _silu(a):
    return a * (1.0 / (1.0 + jnp.exp(-a)))


def _softplus(a):
    return jnp.maximum(a, 0.0) + jnp.log1p(jnp.exp(-jnp.abs(a)))


def _mod_kernel(c_ref, w_ref, b_ref, o_ref):
    cond = _silu(c_ref[...])
    o_ref[0] = _dot3(cond, w_ref[0]) + b_ref[0]


def _modulation(c, w_mod, b_mod):
    depth, d, nm = w_mod.shape
    b = c.shape[0]
    rows = 8
    cp = jnp.zeros((rows, d), F32).at[:b].set(c)
    tn = 1024
    out = pl.pallas_call(
        _mod_kernel,
        grid=(depth, nm // tn),
        in_specs=[
            pl.BlockSpec((rows, d), lambda l, j: (0, 0)),
            pl.BlockSpec((1, d, tn), lambda l, j: (l, 0, j)),
            pl.BlockSpec((1, 1, tn), lambda l, j: (l, 0, j)),
        ],
        out_specs=pl.BlockSpec((1, rows, tn), lambda l, j: (l, 0, j)),
        out_shape=jax.ShapeDtypeStruct((depth, rows, nm), F32),
        compiler_params=_cparams(("arbitrary", "arbitrary")),
        name="modulation",
    )(cp, w_mod, b_mod.reshape(depth, 1, nm))
    return out[:, :b]


def _inproj_kernel(x_ref, g_ref, sh_ref, sc_ref, w_ref, o_ref, h_scr):
    @pl.when(pl.program_id(1) == 0)
    def _():
        h = _rms(x_ref[...], g_ref[...]) * (1.0 + sc_ref[0]) + sh_ref[0]
        h_scr[...] = h.astype(BF16)

    o_ref[...] = _dot(h_scr[...], w_ref[...]).astype(o_ref.dtype)


def _in_projection(x2, g, mod6, w_in_p, seq):
    t, d = x2.shape
    tm = min(1024, seq)
    tn = 512
    per_b = seq // tm
    return pl.pallas_call(
        _inproj_kernel,
        grid=(t // tm, NP // tn),
        in_specs=[
            pl.BlockSpec((tm, d), lambda i, j: (i, 0)),
            pl.BlockSpec((1, d), lambda i, j: (0, 0)),
            pl.BlockSpec((1, 1, d), lambda i, j: ((i // per_b) * N_MOD + 0, 0, 0)),
            pl.BlockSpec((1, 1, d), lambda i, j: ((i // per_b) * N_MOD + 1, 0, 0)),
            pl.BlockSpec((d, tn), lambda i, j: (0, j)),
        ],
        out_specs=pl.BlockSpec((tm, tn), lambda i, j: (i, j)),
        out_shape=jax.ShapeDtypeStruct((t, NP), PROJ_DTYPE),
        scratch_shapes=[pltpu.VMEM((tm, d), BF16)],
        compiler_params=_cparams(("arbitrary", "arbitrary")),
        name="adaln_inproj",
    )(x2, g.reshape(1, d), mod6, mod6, w_in_p)


def _mixa_kernel(p_ref, halo_ref, w_ref, g_ref, o_ref, ext_scr):
    ts = p_ref.shape[0]
    w = GROUP_W
    p = p_ref[...].astype(F32)
    hp = halo_ref[...].astype(F32)
    u = p[:, w:2 * w] * p[:, 2 * w:3 * w]
    hu = hp[:, w:2 * w] * hp[:, 2 * w:3 * w]
    hu = jnp.where(pl.program_id(1) == 0, 0.0, hu)
    ext_scr[0:HALO, :] = hu
    ext_scr[HALO:HALO + ts, :] = u
    cw = w_ref[...]
    conv = (cw[0:1] * ext_scr[HALO - 2:HALO - 2 + ts, :]
            + cw[1:2] * ext_scr[HALO - 1:HALO - 1 + ts, :]
            + cw[2:3] * u)
    o_ref[...] = _rms(p[:, 0:w] * conv, g_ref[...]).astype(o_ref.dtype)


def _mixer_a(proj, conv_w, gn, batch, seq):
    t = proj.shape[0]
    ts = min(512, seq)
    nt = seq // ts
    hb = ts // HALO
    wa = 3 * GROUP_W
    return pl.pallas_call(
        _mixa_kernel,
        grid=(batch, nt),
        in_specs=[
            pl.BlockSpec((ts, wa), lambda b, i: (b * nt + i, U_A * LANE // wa)),
            pl.BlockSpec((HALO, wa), lambda b, i: (jnp.maximum((b * nt + i) * hb - 1, 0), U_A * LANE // wa)),
            pl.BlockSpec(conv_w.shape, lambda b, i: (0, 0)),
            pl.BlockSpec((1, GROUP_W), lambda b, i: (0, 0)),
        ],
        out_specs=pl.BlockSpec((ts, GROUP_W), lambda b, i: (b * nt + i, 0)),
        out_shape=jax.ShapeDtypeStruct((t, GROUP_W), BF16),
        scratch_shapes=[pltpu.VMEM((HALO + ts, GROUP_W), F32)],
        compiler_params=_cparams(("arbitrary", "arbitrary")),
        name="mixer_a_conv",
    )(proj, proj, conv_w, gn.reshape(1, GROUP_W))


def _swa_kernel(q_ref, k_ref, kp_ref, v_ref, vp_ref, sink_ref, slope_ref, g_ref, o_ref, y_scr):
    w = SWA_WINDOW
    hd = HEAD_DIM
    grp = N_HEADS // SWA_KV_HEADS
    first = pl.program_id(1) == 0
    q = q_ref[...].astype(BF16)
    k = jnp.concatenate([kp_ref[...], k_ref[...]], axis=0).astype(BF16)
    v = jnp.concatenate([vp_ref[...], v_ref[...]], axis=0).astype(BF16)
    qi = lax.broadcasted_iota(jnp.int32, (w, 2 * w), 0)
    kj = lax.broadcasted_iota(jnp.int32, (w, 2 * w), 1)
    dist = qi + w - kj
    valid = (dist >= 0) & (dist < w) & ((kj >= w) | jnp.logical_not(first))
    distf = dist.astype(F32)
    sinks = sink_ref[...]
    slopes = slope_ref[...]
    for kvh in range(SWA_KV_HEADS):
        kh = k[:, kvh * hd:(kvh + 1) * hd]
        vh = v[:, kvh * hd:(kvh + 1) * hd]
        for gi in range(grp):
            h = kvh * grp + gi
            qh = q[:, h * hd:(h + 1) * hd]
            s = lax.dot_general(qh, kh, (((1,), (1,)), ((), ())), preferred_element_type=F32)
            s = s * (1.0 / math.sqrt(hd)) - slopes[0:1, h:h + 1] * distf
            s = jnp.where(valid, s, -jnp.inf)
            sink = sinks[0:1, h:h + 1]
            m = jnp.maximum(jnp.max(s, axis=-1, keepdims=True), sink)
            e = jnp.exp(s - m)
            denom = jnp.sum(e, axis=-1, keepdims=True) + jnp.exp(sink - m)
            p = e * (1.0 / denom)
            y_scr[:, h * hd:(h + 1) * hd] = _dot(p.astype(BF16), vh)
    o_ref[...] = _rms(y_scr[...], g_ref[...]).astype(o_ref.dtype)


def _mixer_b(proj, sinks, gn, batch, seq):
    t = proj.shape[0]
    w = SWA_WINDOW
    nb = seq // w
    slopes = 2.0 ** (-8.0 * jnp.arange(1, N_HEADS + 1, dtype=F32) / N_HEADS)
    pad = lambda a: jnp.zeros((1, LANE), F32).at[0, :N_HEADS].set(a)
    cur = lambda u: (lambda b, n: (b * nb + n, u))
    prev = lambda u: (lambda b, n: (b * nb + jnp.maximum(n - 1, 0), u))
    return pl.pallas_call(
        _swa_kernel,
        grid=(batch, nb),
        in_specs=[
            pl.BlockSpec((w, GROUP_W), cur(U_BQ * LANE // GROUP_W)),
            pl.BlockSpec((w, LANE), cur(U_BK)),
            pl.BlockSpec((w, LANE), prev(U_BK)),
            pl.BlockSpec((w, LANE), cur(U_BV)),
            pl.BlockSpec((w, LANE), prev(U_BV)),
            pl.BlockSpec((1, LANE), lambda b, n: (0, 0)),
            pl.BlockSpec((1, LANE), lambda b, n: (0, 0)),
            pl.BlockSpec((1, GROUP_W), lambda b, n: (0, 0)),
        ],
        out_specs=pl.BlockSpec((w, GROUP_W), lambda b, n: (b * nb + n, 0)),
        out_shape=jax.ShapeDtypeStruct((t, GROUP_W), BF16),
        scratch_shapes=[pltpu.VMEM((w, GROUP_W), F32)],
        compiler_params=_cparams(("arbitrary", "arbitrary")),
        name="mixer_b_swa",
    )(proj, proj, proj, proj, proj, pad(sinks), pad(slopes), gn.reshape(1, GROUP_W))


def _ssd_kernel(z_ref, xbc_ref, halo_ref, dt_ref, cw_ref, cb_ref, dtb_ref, alog_ref, dsk_ref, g_ref,
                o_ref, ext_scr, y_scr, state_scr):
    l = SSM_CHUNK
    hd = HEAD_DIM
    n = SSM_STATE
    rep = N_HEADS // SSM_GROUPS
    first = pl.program_id(1) == 0

    @pl.when(first)
    def _():
        state_scr[...] = jnp.zeros_like(state_scr)

    raw = xbc_ref[...].astype(F32)
    halo = jnp.where(first, 0.0, halo_ref[...].astype(F32))
    ext_scr[0:HALO, :] = halo
    ext_scr[HALO:HALO + l, :] = raw
    cw = cw_ref[...]
    conv = (cw[0:1] * ext_scr[HALO - 3:HALO - 3 + l, :]
            + cw[1:2] * ext_scr[HALO - 2:HALO - 2 + l, :]
            + cw[2:3] * ext_scr[HALO - 1:HALO - 1 + l, :]
            + cw[3:4] * raw)
    xbc = _silu(conv + cb_ref[...])
    xs = xbc[:, 0:GROUP_W]
    bm = xbc[:, GROUP_W:GROUP_W + SSM_GROUPS * n]
    cm = xbc[:, GROUP_W + SSM_GROUPS * n:GROUP_W + 2 * SSM_GROUPS * n]

    dt = _softplus(dt_ref[...].astype(F32) + dtb_ref[...])
    da = dt * (-jnp.exp(alog_ref[...]))
    ri = lax.broadcasted_iota(jnp.int32, (l, l), 0)
    ci = lax.broadcasted_iota(jnp.int32, (l, l), 1)
    causal = ri >= ci
    tril = jnp.where(causal, 1.0, 0.0).astype(BF16)
    da_hi, da_lo = _split_bf16(da)
    cum = _dot(tril, da_hi) + _dot(tril, da_lo)
    cum_t = cum.T
    last = cum[l - 1:l, :]
    dsk = dsk_ref[...]

    for gi in range(SSM_GROUPS):
        bg = bm[:, gi * n:(gi + 1) * n]
        cg = cm[:, gi * n:(gi + 1) * n].astype(BF16)
        bg_t = bg.T.astype(BF16)
        cb = _dot(cg, bg_t)
        for hi_ in range(rep):
            h = gi * rep + hi_
            col = cum[:, h:h + 1]
            row = cum_t[h:h + 1, :]
            decay = jnp.exp(jnp.where(causal, col - row, -jnp.inf))
            xh = xs[:, h * hd:(h + 1) * hd]
            xdt = xh * dt[:, h:h + 1]
            y = _dot((cb * decay).astype(BF16), xdt.astype(BF16))
            prev = state_scr[h]
            y = y + _dot(cg, prev.astype(BF16)) * jnp.exp(col)
            y = y + xh * dsk[0:1, h:h + 1]
            y_scr[:, h * hd:(h + 1) * hd] = y
            lh = last[0:1, h:h + 1]
            to_end = jnp.exp(lh - col)
            new = _dot(bg_t, (xdt * to_end).astype(BF16))
            state_scr[h] = prev * jnp.exp(lh) + new

    y = y_scr[...] * _silu(z_ref[...].astype(F32))
    o_ref[...] = _rms(y, g_ref[...]).astype(o_ref.dtype)


def _mixer_c(proj, conv_w, conv_b, dt_bias, a_log, d_skip, norm_g, batch, seq):
    t = proj.shape[0]
    l = SSM_CHUNK
    nc = seq // l
    wx = GROUP_W + 2 * SSM_GROUPS * SSM_STATE
    hb = l // HALO
    pad = lambda a: jnp.zeros((1, LANE), F32).at[0, :N_HEADS].set(a)
    cur = lambda u: (lambda b, c: (b * nc + c, u))
    return pl.pallas_call(
        _ssd_kernel,
        grid=(batch, nc),
        in_specs=[
            pl.BlockSpec((l, GROUP_W), cur(U_Z * LANE // GROUP_W)),
            pl.BlockSpec((l, wx), cur(U_XBC * LANE // wx)),
            pl.BlockSpec((HALO, wx), lambda b, c: (jnp.maximum((b * nc + c) * hb - 1, 0), U_XBC * LANE // wx)),
            pl.BlockSpec((l, LANE), cur(U_DT)),
            pl.BlockSpec(conv_w.shape, lambda b, c: (0, 0)),
            pl.BlockSpec((1, wx), lambda b, c: (0, 0)),
            pl.BlockSpec((1, LANE), lambda b, c: (0, 0)),
            pl.BlockSpec((1, LANE), lambda b, c: (0, 0)),
            pl.BlockSpec((1, LANE), lambda b, c: (0, 0)),
            pl.BlockSpec((1, GROUP_W), lambda b, c: (0, 0)),
        ],
        out_specs=pl.BlockSpec((l, GROUP_W), lambda b, c: (b * nc + c, 0)),
        out_shape=jax.ShapeDtypeStruct((t, GROUP_W), BF16),
        scratch_shapes=[
            pltpu.VMEM((HALO + l, wx), F32),
            pltpu.VMEM((l, GROUP_W), F32),
            pltpu.VMEM((N_HEADS, SSM_STATE, HEAD_DIM), F32),
        ],
        compiler_params=_cparams(("arbitrary", "arbitrary")),
        name="mixer_c_ssd",
    )(proj, proj, proj, proj, conv_w, conv_b.reshape(1, wx), pad(dt_bias), pad(a_log), pad(d_skip),
      norm_g.reshape(1, GROUP_W))


def _sb_kernel(q_ref, k_ref, v_ref, o_ref):
    tq = q_ref.shape[0]
    hd = HEAD_DIM
    qi = pl.program_id(2)
    ri = lax.broadcasted_iota(jnp.int32, (tq, tq), 0)
    ci = lax.broadcasted_iota(jnp.int32, (tq, tq), 1)
    strict = ci < ri
    later = jnp.where(ri > ci, 1.0, 0.0).astype(BF16)

    for hh in range(LANE // hd):
        lanes = slice(hh * hd, (hh + 1) * hd)
        q = q_ref[:, lanes].astype(BF16)

        def block(kb, carry, masked):
            run, acc = carry
            rows = pl.ds(pl.multiple_of(kb * tq, tq), tq)
            k = k_ref[rows, lanes].astype(BF16)
            v = v_ref[rows, lanes].astype(BF16)
            z = lax.dot_general(q, k, (((1,), (1,)), ((), ())), preferred_element_type=F32)
            z = z * (1.0 / math.sqrt(hd))
            sp = _softplus(z)
            if masked:
                sp = jnp.where(strict, sp, 0.0)
            sp_hi, sp_lo = _split_bf16(sp)
            after = _dot(sp_hi, later) + _dot(sp_lo, later)
            a = jnp.exp(z - sp - after - run)
            if masked:
                a = jnp.where(strict, a, 0.0)
            acc = acc + _dot(a.astype(BF16), v)
            run = run + jnp.sum(sp, axis=-1, keepdims=True)
            return run, acc

        carry = (jnp.zeros((tq, 1), F32), jnp.zeros((tq, hd), F32))
        carry = block(qi, carry, True)
        carry = lax.fori_loop(0, qi, lambda j, c: block(qi - 1 - j, c, False), carry)
        o_ref[:, lanes] = carry[1].astype(o_ref.dtype)


def _mixer_d(proj, batch, seq):
    t = proj.shape[0]
    tq = min(256, seq)
    nq = seq // tq
    pairs = GROUP_W // LANE
    return pl.pallas_call(
        _sb_kernel,
        grid=(batch, pairs, nq),
        in_specs=[
            pl.BlockSpec((tq, LANE), lambda b, p, i: (b * nq + i, U_DQ + p)),
            pl.BlockSpec((seq, LANE), lambda b, p, i: (b, U_DK + p)),
            pl.BlockSpec((seq, LANE), lambda b, p, i: (b, U_DV + p)),
        ],
        out_specs=pl.BlockSpec((tq, LANE), lambda b, p, i: (b * nq + i, p)),
        out_shape=jax.ShapeDtypeStruct((t, GROUP_W), BF16),
        compiler_params=_cparams(("arbitrary", "arbitrary", "arbitrary")),
        name="mixer_d_stickbreaking",
    )(proj, proj, proj)


def _outproj_kernel(ya_ref, yb_ref, yc_ref, yd_ref, gnd_ref, w_ref, x_ref, gate_ref, sh_ref, sc_ref, g_ref,
                    rw_ref, xo_ref, h_ref, lg_ref):
    w = GROUP_W
    yd = _rms(yd_ref[...].astype(F32), gnd_ref[...]).astype(BF16)
    mix = (_dot(ya_ref[...], w_ref[0:w, :]) + _dot(yb_ref[...], w_ref[w:2 * w, :])
           + _dot(yc_ref[...], w_ref[2 * w:3 * w, :]) + _dot(yd, w_ref[3 * w:4 * w, :]))
    x = x_ref[...] + gate_ref[0] * mix
    xo_ref[...] = x
    h = _rms(x, g_ref[...]) * (1.0 + sc_ref[0]) + sh_ref[0]
    h_ref[...] = h
    lg_ref[...] = _dot3(h, rw_ref[...]).T[0:N_EXPERTS, :]


def _out_projection(ya, yb, yc, yd, gn_d, w_out_b, x2, mod6, g2, router_w_p, seq):
    t, d = x2.shape
    tm = min(256, seq)
    per_b = seq // tm
    row = lambda i: (i, 0)
    const = lambda i: (0, 0)
    modspec = lambda m: pl.BlockSpec((1, 1, d), lambda i: ((i // per_b) * N_MOD + m, 0, 0))
    return pl.pallas_call(
        _outproj_kernel,
        grid=(t // tm,),
        in_specs=[
            pl.BlockSpec((tm, GROUP_W), row), pl.BlockSpec((tm, GROUP_W), row),
            pl.BlockSpec((tm, GROUP_W), row), pl.BlockSpec((tm, GROUP_W), row),
            pl.BlockSpec((1, GROUP_W), const),
            pl.BlockSpec((d, d), const),
            pl.BlockSpec((tm, d), row),
            modspec(2), modspec(3), modspec(4),
            pl.BlockSpec((1, d), const),
            pl.BlockSpec((d, LANE), const),
        ],
        out_specs=[
            pl.BlockSpec((tm, d), row),
            pl.BlockSpec((tm, d), row),
            pl.BlockSpec((N_EXPERTS, tm), lambda i: (0, i)),
        ],
        out_shape=[
            jax.ShapeDtypeStruct((t, d), F32),
            jax.ShapeDtypeStruct((t, d), F32),
            jax.ShapeDtypeStruct((N_EXPERTS, t), F32),
        ],
        compiler_params=_cparams(("arbitrary",)),
        name="outproj_adaln_router",
    )(ya, yb, yc, yd, gn_d.reshape(1, GROUP_W), w_out_b, x2, mod6, mod6, mod6, g2.reshape(1, d), router_w_p)


def _route_kernel(lg_ref, bias_ref, oi_ref, of_ref, cnt_ref):
    ne = N_EXPERTS
    rt = lg_ref.shape[1]

    @pl.when(pl.program_id(0) == 0)
    def _():
        cnt_ref[...] = jnp.zeros_like(cnt_ref)

    lg = lg_ref[...]
    e = jnp.exp(lg - jnp.max(lg, axis=0, keepdims=True))
    probs = e / jnp.sum(e, axis=0, keepdims=True)
    sel = probs + bias_ref[...][:, 0:1]
    rowi = lax.broadcasted_iota(jnp.int32, (ne, rt), 0)
    neg = -jnp.inf

    gscore = []
    for g in range(N_EXPERT_GROUPS):
        r = [sel[g * EXPERTS_PER_GROUP + i:g * EXPERTS_PER_GROUP + i + 1, :] for i in range(EXPERTS_PER_GROUP)]
        best = None
        for i in range(EXPERTS_PER_GROUP):
            for j in range(i + 1, EXPERTS_PER_GROUP):
                pair = jnp.maximum(r[i], r[j]) + jnp.minimum(r[i], r[j])
                best = pair if best is None else jnp.maximum(best, pair)
        gscore.append(best)
    top_g = jnp.zeros((1, rt), jnp.int32)
    top_s = gscore[0]
    for g in range(1, N_EXPERT_GROUPS):
        better = gscore[g] > top_s
        top_g = jnp.where(better, g, top_g)
        top_s = jnp.where(better, gscore[g], top_s)

    in_grp = (rowi // EXPERTS_PER_GROUP) == top_g
    masked = jnp.where(in_grp, sel, neg)
    m1 = jnp.max(masked, axis=0, keepdims=True)
    i1 = jnp.min(jnp.where(masked == m1, rowi, ne), axis=0, keepdims=True)
    masked2 = jnp.where(rowi == i1, neg, masked)
    m2 = jnp.max(masked2, axis=0, keepdims=True)
    i2 = jnp.min(jnp.where(masked2 == m2, rowi, ne), axis=0, keepdims=True)
    hit1 = rowi == i1
    hit2 = rowi == i2
    g1 = jnp.sum(jnp.where(hit1, probs, 0.0), axis=0, keepdims=True)
    g2 = jnp.sum(jnp.where(hit2, probs, 0.0), axis=0, keepdims=True)
    gsum = g1 + g2

    onehot = jnp.where(hit1 | hit2, 1.0, 0.0)
    ji = lax.broadcasted_iota(jnp.int32, (rt, rt), 0)
    ti = lax.broadcasted_iota(jnp.int32, (rt, rt), 1)
    before = jnp.where(ji < ti, 1.0, 0.0).astype(BF16)
    base = cnt_ref[...][:, 0:1]
    prior = _dot(onehot.astype(BF16), before) + base
    r1 = jnp.sum(jnp.where(hit1, prior, 0.0), axis=0, keepdims=True)
    r2 = jnp.sum(jnp.where(hit2, prior, 0.0), axis=0, keepdims=True)
    cnt_ref[...] = cnt_ref[...] + jnp.sum(onehot, axis=1, keepdims=True)

    zi = jnp.zeros((4, rt), jnp.int32)
    oi_ref[...] = jnp.concatenate([i1, i2, r1.astype(jnp.int32), r2.astype(jnp.int32), zi], axis=0)
    zf = jnp.zeros((6, rt), F32)
    of_ref[...] = jnp.concatenate([g1 / gsum, g2 / gsum, zf], axis=0)


def _routing(logits_t, router_bias):
    ne, t = logits_t.shape
    rt = min(512, t)
    bias = jnp.broadcast_to(router_bias.astype(F32)[:, None], (ne, LANE))
    return pl.pallas_call(
        _route_kernel,
        grid=(t // rt,),
        in_specs=[pl.BlockSpec((ne, rt), lambda i: (0, i)), pl.BlockSpec((ne, LANE), lambda i: (0, 0))],
        out_specs=[
            pl.BlockSpec((8, rt), lambda i: (0, i)),
            pl.BlockSpec((8, rt), lambda i: (0, i)),
            pl.BlockSpec((ne, LANE), lambda i: (0, 0)),
        ],
        out_shape=[
            jax.ShapeDtypeStruct((8, t), jnp.int32),
            jax.ShapeDtypeStruct((8, t), F32),
            jax.ShapeDtypeStruct((ne, LANE), F32),
        ],
        compiler_params=_cparams(("arbitrary",)),
        name="moe_routing",
    )(logits_t, bias)


def _moe_kernel(blk_e, blk_n, code, h_hbm, wg_ref, wu_ref, wd_ref, out_hbm, xbuf, ybuf, gsem, ssem):
    i = pl.program_id(0)
    nblk = pl.num_programs(0)
    rb = xbuf.shape[1]
    t = h_hbm.shape[0]
    slot = i % 2

    def gather_copy(blk, s, j):
        dst_row = code[blk * rb + j]
        tok = jnp.where(dst_row >= t, dst_row - t, dst_row)
        return pltpu.make_async_copy(h_hbm.at[pl.ds(tok, 1), :], xbuf.at[s, pl.ds(j, 1), :], gsem.at[s])

    def scatter_copy(blk, s, j):
        dst_row = code[blk * rb + j]
        return pltpu.make_async_copy(ybuf.at[s, pl.ds(j, 1), :], out_hbm.at[pl.ds(dst_row, 1), :], ssem.at[s])

    def for_rows(blk, fn):
        def body(j, carry):
            fn(j)
            return carry
        lax.fori_loop(0, blk_n[blk], body, 0)

    @pl.when(i == 0)
    def _():
        xbuf[...] = jnp.zeros_like(xbuf)
        for_rows(0, lambda j: gather_copy(0, 0, j).start())

    @pl.when(i + 1 < nblk)
    def _():
        for_rows(i + 1, lambda j: gather_copy(i + 1, 1 - slot, j).start())

    for_rows(i, lambda j: gather_copy(i, slot, j).wait())

    @pl.when(i >= 2)
    def _():
        for_rows(i - 2, lambda j: scatter_copy(i - 2, slot, j).wait())

    @pl.when(blk_n[i] > 0)
    def _():
        x = xbuf[slot].astype(BF16)
        hid = _silu(_dot(x, wg_ref[0])) * _dot(x, wu_ref[0])
        ybuf[slot] = _dot(hid.astype(BF16), wd_ref[0])
        for_rows(i, lambda j: scatter_copy(i, slot, j).start())

    @pl.when(i == nblk - 1)
    def _():
        @pl.when(i >= 1)
        def _():
            for_rows(i - 1, lambda j: scatter_copy(i - 1, 1 - slot, j).wait())
        for_rows(i, lambda j: scatter_copy(i, slot, j).wait())


def _experts(blk_e, blk_n, code, h2, wg, wu, wd, rb):
    t, d = h2.shape
    ne, _, dff = wg.shape
    nblk = blk_e.shape[0]
    return pl.pallas_call(
        _moe_kernel,
        grid_spec=pltpu.PrefetchScalarGridSpec(
            num_scalar_prefetch=3,
            grid=(nblk,),
            in_specs=[
                pl.BlockSpec(memory_space=pl.ANY),
                pl.BlockSpec((1, d, dff), lambda i, be, bn, cd: (be[i], 0, 0)),
                pl.BlockSpec((1, d, dff), lambda i, be, bn, cd: (be[i], 0, 0)),
                pl.BlockSpec((1, dff, d), lambda i, be, bn, cd: (be[i], 0, 0)),
            ],
            out_specs=pl.BlockSpec(memory_space=pl.ANY),
            scratch_shapes=[
                pltpu.VMEM((2, rb, d), F32),
                pltpu.VMEM((2, rb, d), F32),
                pltpu.SemaphoreType.DMA((2,)),
                pltpu.SemaphoreType.DMA((2,)),
            ],
        ),
        out_shape=jax.ShapeDtypeStruct((2 * t, d), F32),
        compiler_params=_cparams(("arbitrary",)),
        name="moe_experts",
    )(blk_e, blk_n, code, h2, wg, wu, wd)


def _combine_kernel(x_ref, y0_ref, y1_ref, w_ref, gate_ref, g_ref, o_ref, *, final):
    w = w_ref[...]
    moe = w[:, 0:1] * y0_ref[...] + w[:, 1:2] * y1_ref[...]
    x = x_ref[...] + gate_ref[0] * moe
    if final:
        x = _rms(x, g_ref[...])
    o_ref[...] = x


def _combine(x2, y2, gates_t, mod6, final_g, seq, final):
    t, d = x2.shape
    tm = min(512, seq)
    per_b = seq // tm
    nt = t // tm
    return pl.pallas_call(
        functools.partial(_combine_kernel, final=final),
        grid=(nt,),
        in_specs=[
            pl.BlockSpec((tm, d), lambda i: (i, 0)),
            pl.BlockSpec((tm, d), lambda i: (i, 0)),
            pl.BlockSpec((tm, d), lambda i: (nt + i, 0)),
            pl.BlockSpec((tm, 2), lambda i: (i, 0)),
            pl.BlockSpec((1, 1, d), lambda i: ((i // per_b) * N_MOD + 5, 0, 0)),
            pl.BlockSpec((1, d), lambda i: (0, 0)),
        ],
        out_specs=pl.BlockSpec((tm, d), lambda i: (i, 0)),
        out_shape=jax.ShapeDtypeStruct((t, d), F32),
        compiler_params=_cparams(("arbitrary",)),
        name="moe_combine",
    )(x2, y2, y2, gates_t, mod6, final_g.reshape(1, d))


def _permute_w_in(w_in):
    depth, d, _ = w_in.shape
    gw = GROUP_W
    kv = SWA_KV_HEADS * HEAD_DIM
    xbc = gw + 2 * SSM_GROUPS * SSM_STATE
    o = 0
    a = w_in[:, :, o:o + 3 * gw]; o += 3 * gw
    bq = w_in[:, :, o:o + gw]; o += gw
    bk = w_in[:, :, o:o + kv]; o += kv
    bv = w_in[:, :, o:o + kv]; o += kv
    cz = w_in[:, :, o:o + gw]; o += gw
    cx = w_in[:, :, o:o + xbc]; o += xbc
    cdt = w_in[:, :, o:o + N_HEADS]; o += N_HEADS
    dqkv = w_in[:, :, o:o + 3 * gw]; o += 3 * gw
    zeros = lambda n: jnp.zeros((depth, d, n), w_in.dtype)
    out = jnp.concatenate([a, dqkv, cz, bq, cx, bk, bv, cdt, zeros(LANE - N_HEADS), zeros(LANE)], axis=-1)
    return out.astype(BF16)


def _dispatch_plan(idx, cnt, t, rb):
    ne = N_EXPERTS
    counts = cnt[:, 0].astype(jnp.int32)
    padded = (counts + rb - 1) // rb * rb
    pad_ends = jnp.cumsum(padded)
    pad_starts = pad_ends - padded
    e_idx = idx[0:2]
    rank = idx[2:4]
    onehot = e_idx[:, :, None] == jnp.arange(ne, dtype=jnp.int32)
    dest = jnp.sum(jnp.where(onehot, pad_starts, 0), axis=-1) + rank
    cap = (2 * t + ne * rb) // rb * rb
    nblk = cap // rb
    dst_code = jnp.arange(2, dtype=jnp.int32)[:, None] * t + jnp.arange(t, dtype=jnp.int32)[None, :]
    code = jnp.zeros((cap,), jnp.int32).at[dest.reshape(-1)].set(dst_code.reshape(-1))
    blk_start = jnp.arange(nblk, dtype=jnp.int32) * rb
    blk_e = jnp.minimum(jnp.sum(pad_ends[None, :] <= blk_start[:, None], axis=1), ne - 1).astype(jnp.int32)
    blk_n = jnp.clip(counts[blk_e] - (blk_start - pad_starts[blk_e]), 0, rb).astype(jnp.int32)
    return blk_e, blk_n, code


def kernel(x, c, w_mod, b_mod, norm1_g, norm2_g, w_in, w_out, conv_a_w, gn_a, attn_sinks, gn_b, ssm_conv_w,
           ssm_conv_b, dt_bias, a_log, d_skip, ssm_norm_g, gn_d, router_w, router_bias, moe_w_gate, moe_w_up,
           moe_w_down, final_g):
    batch, seq, d = x.shape
    depth = w_in.shape[0]
    t = batch * seq
    rb = 256

    mod = _modulation(c, w_mod, b_mod)
    w_in_p = _permute_w_in(w_in)
    w_out_b = w_out.astype(BF16)
    router_w_p = jnp.zeros((d, LANE), F32).at[:, :N_EXPERTS].set(router_w)

    x2 = x.reshape(t, d)
    for l in range(depth):
        mod6 = mod[l].reshape(batch * N_MOD, 1, d)
        proj = _in_projection(x2, norm1_g[l], mod6, w_in_p[l], seq)
        ya = _mixer_a(proj, conv_a_w[l], gn_a[l], batch, seq)
        yb = _mixer_b(proj, attn_sinks[l], gn_b[l], batch, seq)
        yc = _mixer_c(proj, ssm_conv_w[l], ssm_conv_b[l], dt_bias[l], a_log[l], d_skip[l], ssm_norm_g[l],
                      batch, seq)
        yd = _mixer_d(proj, batch, seq)
        x2, h2, logits_t = _out_projection(ya, yb, yc, yd, gn_d[l], w_out_b[l], x2, mod6, norm2_g[l],
                                           router_w_p, seq)
        idx, gates, cnt = _routing(logits_t, router_bias)
        blk_e, blk_n, code = _dispatch_plan(idx, cnt, t, rb)
        y2 = _experts(blk_e, blk_n, code, h2, moe_w_gate[l].astype(BF16), moe_w_up[l].astype(BF16),
                      moe_w_down[l].astype(BF16), rb)
        x2 = _combine(x2, y2, gates[0:2].T, mod6, final_g, seq, final=(l == depth - 1))
    return x2.reshape(batch, seq, d)
```

```python
import functools
import math

import jax
import jax.numpy as jnp
from jax import lax
from jax.experimental import pallas as pl
from jax.experimental.pallas import tpu as pltpu

F32 = jnp.float32
BF16 = jnp.bfloat16

EPS = 1e-6
HEAD_DIM = 64
GROUP_W = 512
N_HEADS = GROUP_W // HEAD_DIM
SWA_KV_HEADS = 2
SWA_WINDOW = 128
SSM_GROUPS = 2
SSM_STATE = 128
SSM_CHUNK = 128
N_EXPERTS = 16
N_EXPERT_GROUPS = 4
EXPERTS_PER_GROUP = N_EXPERTS // N_EXPERT_GROUPS
N_MOD = 6

LANE = 128
HALO = 16
SB_KEYS = 256
VMEM_LIMIT = 56 * 1024 * 1024

U_A = 0
U_DQ = 12
U_DK = 16
U_DV = 20
U_Z = 24
U_BQ = 28
U_XBC = 32
U_BK = 40
U_BV = 41
U_DT = 42
N_UNITS = 44
NP = N_UNITS * LANE

PROJ_DTYPE = jnp.float32


def _cparams(sem, vmem=VMEM_LIMIT):
    return pltpu.CompilerParams(dimension_semantics=sem, vmem_limit_bytes=vmem)


def _rms(y, g):
    return y * lax.rsqrt(jnp.mean(y * y, axis=-1, keepdims=True) + EPS) * g


def _split_bf16(a):
    hi = a.astype(BF16)
    lo = (a - hi.astype(F32)).astype(BF16)
    return hi, lo


def _dot(a, b):
    return jnp.dot(a, b, preferred_element_type=F32)


def _dot3(a, b):
    ah, al = _split_bf16(a)
    bh, bl = _split_bf16(b)
    return _dot(ah, bh) + _dot(al, bh) + _dot(ah, bl)


def _silu(a):
    return a * (1.0 / (1.0 + jnp.exp(-a)))


def _softplus(a):
    return jnp.maximum(a, 0.0) + jnp.log1p(jnp.exp(-jnp.abs(a)))


def _mod_kernel(c_ref, w_ref, b_ref, o_ref):
    cond = _silu(c_ref[...])
    o_ref[0] = _dot3(cond, w_ref[0]) + b_ref[0]


def _modulation(c, w_mod, b_mod):
    depth, d, nm = w_mod.shape
    b = c.shape[0]
    rows = 8
    cp = jnp.zeros((rows, d), F32).at[:b].set(c)
    tn = 1024
    out = pl.pallas_call(
        _mod_kernel,
        grid=(depth, nm // tn),
        in_specs=[
            pl.BlockSpec((rows, d), lambda l, j: (0, 0)),
            pl.BlockSpec((1, d, tn), lambda l, j: (l, 0, j)),
            pl.BlockSpec((1, 1, tn), lambda l, j: (l, 0, j)),
        ],
        out_specs=pl.BlockSpec((1, rows, tn), lambda l, j: (l, 0, j)),
        out_shape=jax.ShapeDtypeStruct((depth, rows, nm), F32),
        compiler_params=_cparams(("arbitrary", "arbitrary")),
        name="modulation",
    )(cp, w_mod, b_mod.reshape(depth, 1, nm))
    return out[:, :b]


def _inproj_kernel(x_ref, g_ref, sh_ref, sc_ref, w_ref, o_ref, h_scr):
    @pl.when(pl.program_id(1) == 0)
    def _():
        h = _rms(x_ref[...], g_ref[...]) * (1.0 + sc_ref[0]) + sh_ref[0]
        h_scr[...] = h.astype(BF16)

    o_ref[...] = _dot(h_scr[...], w_ref[...]).astype(o_ref.dtype)


def _in_projection(x2, g, mod6, w_in_p, seq):
    t, d = x2.shape
    tm = min(1024, seq)
    tn = 512
    per_b = seq // tm
    return pl.pallas_call(
        _inproj_kernel,
        grid=(t // tm, NP // tn),
        in_specs=[
            pl.BlockSpec((tm, d), lambda i, j: (i, 0)),
            pl.BlockSpec((1, d), lambda i, j: (0, 0)),
            pl.BlockSpec((1, 1, d), lambda i, j: ((i // per_b) * N_MOD + 0, 0, 0)),
            pl.BlockSpec((1, 1, d), lambda i, j: ((i // per_b) * N_MOD + 1, 0, 0)),
            pl.BlockSpec((d, tn), lambda i, j: (0, j)),
        ],
        out_specs=pl.BlockSpec((tm, tn), lambda i, j: (i, j)),
        out_shape=jax.ShapeDtypeStruct((t, NP), PROJ_DTYPE),
        scratch_shapes=[pltpu.VMEM((tm, d), BF16)],
        compiler_params=_cparams(("arbitrary", "arbitrary")),
        name="adaln_inproj",
    )(x2, g.reshape(1, d), mod6, mod6, w_in_p)


def _mixa_kernel(p_ref, halo_ref, w_ref, g_ref, o_ref, ext_scr):
    ts = p_ref.shape[0]
    w = GROUP_W
    p = p_ref[...].astype(F32)
    hp = halo_ref[...].astype(F32)
    u = p[:, w:2 * w] * p[:, 2 * w:3 * w]
    hu = hp[:, w:2 * w] * hp[:, 2 * w:3 * w]
    hu = jnp.where(pl.program_id(1) == 0, 0.0, hu)
    ext_scr[0:HALO, :] = hu
    ext_scr[HALO:HALO + ts, :] = u
    cw = w_ref[...]
    conv = (cw[0:1] * ext_scr[HALO - 2:HALO - 2 + ts, :]
            + cw[1:2] * ext_scr[HALO - 1:HALO - 1 + ts, :]
            + cw[2:3] * u)
    o_ref[...] = _rms(p[:, 0:w] * conv, g_ref[...]).astype(o_ref.dtype)


def _mixer_a(proj, conv_w, gn, batch, seq):
    t = proj.shape[0]
    ts = min(512, seq)
    nt = seq // ts
    hb = ts // HALO
    wa = 3 * GROUP_W
    return pl.pallas_call(
        _mixa_kernel,
        grid=(batch, nt),
        in_specs=[
            pl.BlockSpec((ts, wa), lambda b, i: (b * nt + i, U_A * LANE // wa)),
            pl.BlockSpec((HALO, wa), lambda b, i: (jnp.maximum((b * nt + i) * hb - 1, 0), U_A * LANE // wa)),
            pl.BlockSpec(conv_w.shape, lambda b, i: (0, 0)),
            pl.BlockSpec((1, GROUP_W), lambda b, i: (0, 0)),
        ],
        out_specs=pl.BlockSpec((ts, GROUP_W), lambda b, i: (b * nt + i, 0)),
        out_shape=jax.ShapeDtypeStruct((t, GROUP_W), BF16),
        scratch_shapes=[pltpu.VMEM((HALO + ts, GROUP_W), F32)],
        compiler_params=_cparams(("arbitrary", "arbitrary")),
        name="mixer_a_conv",
    )(proj, proj, conv_w, gn.reshape(1, GROUP_W))


def _swa_kernel(q_ref, k_ref, kp_ref, v_ref, vp_ref, sink_ref, slope_ref, g_ref, o_ref, y_scr):
    w = SWA_WINDOW
    hd = HEAD_DIM
    grp = N_HEADS // SWA_KV_HEADS
    first = pl.program_id(1) == 0
    q = q_ref[...].astype(BF16)
    k = jnp.concatenate([kp_ref[...], k_ref[...]], axis=0).astype(BF16)
    v = jnp.concatenate([vp_ref[...], v_ref[...]], axis=0).astype(BF16)
    qi = lax.broadcasted_iota(jnp.int32, (w, 2 * w), 0)
    kj = lax.broadcasted_iota(jnp.int32, (w, 2 * w), 1)
    dist = qi + w - kj
    valid = (dist >= 0) & (dist < w) & ((kj >= w) | jnp.logical_not(first))
    distf = dist.astype(F32)
    sinks = sink_ref[...]
    slopes = slope_ref[...]
    for kvh in range(SWA_KV_HEADS):
        kh = k[:, kvh * hd:(kvh + 1) * hd]
        vh = v[:, kvh * hd:(kvh + 1) * hd]
        for gi in range(grp):
            h = kvh * grp + gi
            qh = q[:, h * hd:(h + 1) * hd]
            s = lax.dot_general(qh, kh, (((1,), (1,)), ((), ())), preferred_element_type=F32)
            s = s * (1.0 / math.sqrt(hd)) - slopes[0:1, h:h + 1] * distf
            s = jnp.where(valid, s, -jnp.inf)
            sink = sinks[0:1, h:h + 1]
            m = jnp.maximum(jnp.max(s, axis=-1, keepdims=True), sink)
            e = jnp.exp(s - m)
            denom = jnp.sum(e, axis=-1, keepdims=True) + jnp.exp(sink - m)
            p = e * (1.0 / denom)
            y_scr[:, h * hd:(h + 1) * hd] = _dot(p.astype(BF16), vh)
    o_ref[...] = _rms(y_scr[...], g_ref[...]).astype(o_ref.dtype)


def _mixer_b(proj, sinks, gn, batch, seq):
    t = proj.shape[0]
    w = SWA_WINDOW
    nb = seq // w
    slopes = 2.0 ** (-8.0 * jnp.arange(1, N_HEADS + 1, dtype=F32) / N_HEADS)
    pad = lambda a: jnp.zeros((1, LANE), F32).at[0, :N_HEADS].set(a)
    cur = lambda u: (lambda b, n: (b * nb + n, u))
    prev = lambda u: (lambda b, n: (b * nb + jnp.maximum(n - 1, 0), u))
    return pl.pallas_call(
        _swa_kernel,
        grid=(batch, nb),
        in_specs=[
            pl.BlockSpec((w, GROUP_W), cur(U_BQ * LANE // GROUP_W)),
            pl.BlockSpec((w, LANE), cur(U_BK)),
            pl.BlockSpec((w, LANE), prev(U_BK)),
            pl.BlockSpec((w, LANE), cur(U_BV)),
            pl.BlockSpec((w, LANE), prev(U_BV)),
            pl.BlockSpec((1, LANE), lambda b, n: (0, 0)),
            pl.BlockSpec((1, LANE), lambda b, n: (0, 0)),
            pl.BlockSpec((1, GROUP_W), lambda b, n: (0, 0)),
        ],
        out_specs=pl.BlockSpec((w, GROUP_W), lambda b, n: (b * nb + n, 0)),
        out_shape=jax.ShapeDtypeStruct((t, GROUP_W), BF16),
        scratch_shapes=[pltpu.VMEM((w, GROUP_W), F32)],
        compiler_params=_cparams(("arbitrary", "arbitrary")),
        name="mixer_b_swa",
    )(proj, proj, proj, proj, proj, pad(sinks), pad(slopes), gn.reshape(1, GROUP_W))


def _ssd_kernel(z_ref, xbc_ref, halo_ref, dt_ref, cw_ref, cb_ref, dtb_ref, alog_ref, dsk_ref, g_ref,
                o_ref, ext_scr, y_scr, state_scr):
    l = SSM_CHUNK
    hd = HEAD_DIM
    n = SSM_STATE
    rep = N_HEADS // SSM_GROUPS
    first = pl.program_id(1) == 0

    @pl.when(first)
    def _():
        state_scr[...] = jnp.zeros_like(state_scr)

    raw = xbc_ref[...].astype(F32)
    halo = jnp.where(first, 0.0, halo_ref[...].astype(F32))
    ext_scr[0:HALO, :] = halo
    ext_scr[HALO:HALO + l, :] = raw
    cw = cw_ref[...]
    conv = (cw[0:1] * ext_scr[HALO - 3:HALO - 3 + l, :]
            + cw[1:2] * ext_scr[HALO - 2:HALO - 2 + l, :]
            + cw[2:3] * ext_scr[HALO - 1:HALO - 1 + l, :]
            + cw[3:4] * raw)
    xbc = _silu(conv + cb_ref[...])
    xs = xbc[:, 0:GROUP_W]
    bm = xbc[:, GROUP_W:GROUP_W + SSM_GROUPS * n]
    cm = xbc[:, GROUP_W + SSM_GROUPS * n:GROUP_W + 2 * SSM_GROUPS * n]

    dt = _softplus(dt_ref[...].astype(F32) + dtb_ref[...])
    da = dt * (-jnp.exp(alog_ref[...]))
    ri = lax.broadcasted_iota(jnp.int32, (l, l), 0)
    ci = lax.broadcasted_iota(jnp.int32, (l, l), 1)
    causal = ri >= ci
    tril = jnp.where(causal, 1.0, 0.0).astype(BF16)
    da_hi, da_lo = _split_bf16(da)
    cum = _dot(tril, da_hi) + _dot(tril, da_lo)
    cum_t = cum.T
    last = cum[l - 1:l, :]
    dsk = dsk_ref[...]

    for gi in range(SSM_GROUPS):
        bg = bm[:, gi * n:(gi + 1) * n]
        cg = cm[:, gi * n:(gi + 1) * n].astype(BF16)
        bg_t = bg.T.astype(BF16)
        cb = _dot(cg, bg_t)
        for hi_ in range(rep):
            h = gi * rep + hi_
            col = cum[:, h:h + 1]
            row = cum_t[h:h + 1, :]
            decay = jnp.exp(jnp.where(causal, col - row, -jnp.inf))
            xh = xs[:, h * hd:(h + 1) * hd]
            xdt = xh * dt[:, h:h + 1]
            y = _dot((cb * decay).astype(BF16), xdt.astype(BF16))
            prev = state_scr[h]
            y = y + _dot(cg, prev.astype(BF16)) * jnp.exp(col)
            y = y + xh * dsk[0:1, h:h + 1]
            y_scr[:, h * hd:(h + 1) * hd] = y
            lh = last[0:1, h:h + 1]
            to_end = jnp.exp(lh - col)
            new = _dot(bg_t, (xdt * to_end).astype(BF16))
            state_scr[h] = prev * jnp.exp(lh) + new

    y = y_scr[...] * _silu(z_ref[...].astype(F32))
    o_ref[...] = _rms(y, g_ref[...]).astype(o_ref.dtype)


def _mixer_c(proj, conv_w, conv_b, dt_bias, a_log, d_skip, norm_g, batch, seq):
    t = proj.shape[0]
    l = SSM_CHUNK
    nc = seq // l
    wx = GROUP_W + 2 * SSM_GROUPS * SSM_STATE
    hb = l // HALO
    pad = lambda a: jnp.zeros((1, LANE), F32).at[0, :N_HEADS].set(a)
    cur = lambda u: (lambda b, c: (b * nc + c, u))
    return pl.pallas_call(
        _ssd_kernel,
        grid=(batch, nc),
        in_specs=[
            pl.BlockSpec((l, GROUP_W), cur(U_Z * LANE // GROUP_W)),
            pl.BlockSpec((l, wx), cur(U_XBC * LANE // wx)),
            pl.BlockSpec((HALO, wx), lambda b, c: (jnp.maximum((b * nc + c) * hb - 1, 0), U_XBC * LANE // wx)),
            pl.BlockSpec((l, LANE), cur(U_DT)),
            pl.BlockSpec(conv_w.shape, lambda b, c: (0, 0)),
            pl.BlockSpec((1, wx), lambda b, c: (0, 0)),
            pl.BlockSpec((1, LANE), lambda b, c: (0, 0)),
            pl.BlockSpec((1, LANE), lambda b, c: (0, 0)),
            pl.BlockSpec((1, LANE), lambda b, c: (0, 0)),
            pl.BlockSpec((1, GROUP_W), lambda b, c: (0, 0)),
        ],
        out_specs=pl.BlockSpec((l, GROUP_W), lambda b, c: (b * nc + c, 0)),
        out_shape=jax.ShapeDtypeStruct((t, GROUP_W), BF16),
        scratch_shapes=[
            pltpu.VMEM((HALO + l, wx), F32),
            pltpu.VMEM((l, GROUP_W), F32),
            pltpu.VMEM((N_HEADS, SSM_STATE, HEAD_DIM), F32),
        ],
        compiler_params=_cparams(("arbitrary", "arbitrary")),
        name="mixer_c_ssd",
    )(proj, proj, proj, proj, conv_w, conv_b.reshape(1, wx), pad(dt_bias), pad(a_log), pad(d_skip),
      norm_g.reshape(1, GROUP_W))


def _sb_kernel(q_ref, k_ref, v_ref, o_ref):
    kb_rows = SB_KEYS
    tq = q_ref.shape[0]
    nr = tq // kb_rows
    hd = HEAD_DIM
    nh = LANE // hd
    qi = pl.program_id(2)
    ri = lax.broadcasted_iota(jnp.int32, (kb_rows, kb_rows), 0)
    ci = lax.broadcasted_iota(jnp.int32, (kb_rows, kb_rows), 1)
    strict = ci < ri
    later = jnp.where(ri > ci, 1.0, 0.0).astype(BF16)
    lanes = [slice(h * hd, (h + 1) * hd) for h in range(nh)]
    qscale = math.log2(math.e) / math.sqrt(hd)
    qs = [[(q_ref[r * kb_rows:(r + 1) * kb_rows, ln].astype(F32) * qscale).astype(BF16) for r in range(nr)]
          for ln in lanes]
    sign = jnp.uint32(0x80000000)

    def stages(chains):
        zs = [lax.dot_general(q, k, (((1,), (1,)), ((), ())), preferred_element_type=F32)
              for q, k, _, _, _, _ in chains]
        sps = []
        for (_, _, _, keep, _, _), z in zip(chains, zs):
            neg_abs = lax.bitcast_convert_type(lax.bitcast_convert_type(z, jnp.uint32) | sign, F32)
            sp = jnp.maximum(z, 0.0) + jnp.log(1.0 + jnp.exp2(neg_abs)) * math.log2(math.e)
            sps.append(sp if keep is None else jnp.where(keep, sp, 0.0))
        afters = [_dot(sp.astype(BF16), later) for sp in sps]
        probs = []
        for (_, _, _, keep, _, _), z, sp, after in zip(chains, zs, sps, afters):
            a = jnp.exp2(z - sp - after)
            probs.append((a if keep is None else jnp.where(keep, a, 0.0)).astype(BF16))
        outs = [_dot(a, v) for (_, _, v, _, _, _), a in zip(chains, probs)]
        return [(run + jnp.sum(sp, axis=-1, keepdims=True), acc + jnp.exp2(-run) * o)
                for (_, _, _, _, run, acc), sp, o in zip(chains, sps, outs)]

    def step(kb, carry, kinds):
        rows = pl.ds(pl.multiple_of(kb * kb_rows, kb_rows), kb_rows)
        chains, where = [], []
        for h in range(nh):
            k = k_ref[rows, lanes[h]].astype(BF16)
            v = v_ref[rows, lanes[h]].astype(BF16)
            for r in range(nr):
                if kinds[r] is not None:
                    run, acc = carry[h * nr + r]
                    chains.append((qs[h][r], k, v, strict if kinds[r] == "diag" else None, run, acc))
                    where.append(h * nr + r)
        carry = list(carry)
        for i, rc in zip(where, stages(chains)):
            carry[i] = rc
        return tuple(carry)

    carry = tuple((jnp.zeros((kb_rows, 1), F32), jnp.zeros((kb_rows, hd), F32)) for _ in range(nh * nr))
    for d in reversed(range(nr)):
        kinds = [None if r < d else ("diag" if r == d else "full") for r in range(nr)]
        carry = step(qi * nr + d, carry, kinds)
    nfull = qi * nr
    carry = lax.fori_loop(0, nfull, lambda j, c: step(nfull - 1 - j, c, ["full"] * nr), carry)
    for h in range(nh):
        for r in range(nr):
            o_ref[r * kb_rows:(r + 1) * kb_rows, lanes[h]] = carry[h * nr + r][1].astype(o_ref.dtype)


def _mixer_d(proj, batch, seq):
    t = proj.shape[0]
    tq = min(2 * SB_KEYS, seq)
    nq = seq // tq
    pairs = GROUP_W // LANE
    return pl.pallas_call(
        _sb_kernel,
        grid=(batch, pairs, nq),
        in_specs=[
            pl.BlockSpec((tq, LANE), lambda b, p, i: (b * nq + i, U_DQ + p)),
            pl.BlockSpec((seq, LANE), lambda b, p, i: (b, U_DK + p)),
            pl.BlockSpec((seq, LANE), lambda b, p, i: (b, U_DV + p)),
        ],
        out_specs=pl.BlockSpec((tq, LANE), lambda b, p, i: (b * nq + i, p)),
        out_shape=jax.ShapeDtypeStruct((t, GROUP_W), BF16),
        compiler_params=_cparams(("arbitrary", "arbitrary", "arbitrary")),
        name="mixer_d_stickbreaking",
    )(proj, proj, proj)


def _outproj_kernel(ya_ref, yb_ref, yc_ref, yd_ref, gnd_ref, w_ref, x_ref, gate_ref, sh_ref, sc_ref, g_ref,
                    rw_ref, xo_ref, h_ref, lg_ref):
    w = GROUP_W
    yd = _rms(yd_ref[...].astype(F32), gnd_ref[...]).astype(BF16)
    mix = (_dot(ya_ref[...], w_ref[0:w, :]) + _dot(yb_ref[...], w_ref[w:2 * w, :])
           + _dot(yc_ref[...], w_ref[2 * w:3 * w, :]) + _dot(yd, w_ref[3 * w:4 * w, :]))
    x = x_ref[...] + gate_ref[0] * mix
    xo_ref[...] = x
    h = _rms(x, g_ref[...]) * (1.0 + sc_ref[0]) + sh_ref[0]
    h_ref[...] = h
    lg_ref[...] = _dot3(h, rw_ref[...]).T[0:N_EXPERTS, :]


def _out_projection(ya, yb, yc, yd, gn_d, w_out_b, x2, mod6, g2, router_w_p, seq):
    t, d = x2.shape
    tm = min(256, seq)
    per_b = seq // tm
    row = lambda i: (i, 0)
    const = lambda i: (0, 0)
    modspec = lambda m: pl.BlockSpec((1, 1, d), lambda i: ((i // per_b) * N_MOD + m, 0, 0))
    return pl.pallas_call(
        _outproj_kernel,
        grid=(t // tm,),
        in_specs=[
            pl.BlockSpec((tm, GROUP_W), row), pl.BlockSpec((tm, GROUP_W), row),
            pl.BlockSpec((tm, GROUP_W), row), pl.BlockSpec((tm, GROUP_W), row),
            pl.BlockSpec((1, GROUP_W), const),
            pl.BlockSpec((d, d), const),
            pl.BlockSpec((tm, d), row),
            modspec(2), modspec(3), modspec(4),
            pl.BlockSpec((1, d), const),
            pl.BlockSpec((d, LANE), const),
        ],
        out_specs=[
            pl.BlockSpec((tm, d), row),
            pl.BlockSpec((tm, d), row),
            pl.BlockSpec((N_EXPERTS, tm), lambda i: (0, i)),
        ],
        out_shape=[
            jax.ShapeDtypeStruct((t, d), F32),
            jax.ShapeDtypeStruct((t, d), F32),
            jax.ShapeDtypeStruct((N_EXPERTS, t), F32),
        ],
        compiler_params=_cparams(("arbitrary",)),
        name="outproj_adaln_router",
    )(ya, yb, yc, yd, gn_d.reshape(1, GROUP_W), w_out_b, x2, mod6, mod6, mod6, g2.reshape(1, d), router_w_p)


def _route_kernel(lg_ref, bias_ref, oi_ref, of_ref, cnt_ref):
    ne = N_EXPERTS
    rt = lg_ref.shape[1]

    @pl.when(pl.program_id(0) == 0)
    def _():
        cnt_ref[...] = jnp.zeros_like(cnt_ref)

    lg = lg_ref[...]
    e = jnp.exp(lg - jnp.max(lg, axis=0, keepdims=True))
    probs = e / jnp.sum(e, axis=0, keepdims=True)
    sel = probs + bias_ref[...][:, 0:1]
    rowi = lax.broadcasted_iota(jnp.int32, (ne, rt), 0)
    neg = -jnp.inf

    gscore = []
    for g in range(N_EXPERT_GROUPS):
        r = [sel[g * EXPERTS_PER_GROUP + i:g * EXPERTS_PER_GROUP + i + 1, :] for i in range(EXPERTS_PER_GROUP)]
        best = None
        for i in range(EXPERTS_PER_GROUP):
            for j in range(i + 1, EXPERTS_PER_GROUP):
                pair = jnp.maximum(r[i], r[j]) + jnp.minimum(r[i], r[j])
                best = pair if best is None else jnp.maximum(best, pair)
        gscore.append(best)
    top_g = jnp.zeros((1, rt), jnp.int32)
    top_s = gscore[0]
    for g in range(1, N_EXPERT_GROUPS):
        better = gscore[g] > top_s
        top_g = jnp.where(better, g, top_g)
        top_s = jnp.where(better, gscore[g], top_s)

    in_grp = (rowi // EXPERTS_PER_GROUP) == top_g
    masked = jnp.where(in_grp, sel, neg)
    m1 = jnp.max(masked, axis=0, keepdims=True)
    i1 = jnp.min(jnp.where(masked == m1, rowi, ne), axis=0, keepdims=True)
    masked2 = jnp.where(rowi == i1, neg, masked)
    m2 = jnp.max(masked2, axis=0, keepdims=True)
    i2 = jnp.min(jnp.where(masked2 == m2, rowi, ne), axis=0, keepdims=True)
    hit1 = rowi == i1
    hit2 = rowi == i2
    g1 = jnp.sum(jnp.where(hit1, probs, 0.0), axis=0, keepdims=True)
    g2 = jnp.sum(jnp.where(hit2, probs, 0.0), axis=0, keepdims=True)
    gsum = g1 + g2

    onehot = jnp.where(hit1 | hit2, 1.0, 0.0)
    ji = lax.broadcasted_iota(jnp.int32, (rt, rt), 0)
    ti = lax.broadcasted_iota(jnp.int32, (rt, rt), 1)
    before = jnp.where(ji < ti, 1.0, 0.0).astype(BF16)
    base = cnt_ref[...][:, 0:1]
    prior = _dot(onehot.astype(BF16), before) + base
    r1 = jnp.sum(jnp.where(hit1, prior, 0.0), axis=0, keepdims=True)
    r2 = jnp.sum(jnp.where(hit2, prior, 0.0), axis=0, keepdims=True)
    cnt_ref[...] = cnt_ref[...] + jnp.sum(onehot, axis=1, keepdims=True)

    zi = jnp.zeros((4, rt), jnp.int32)
    oi_ref[...] = jnp.concatenate([i1, i2, r1.astype(jnp.int32), r2.astype(jnp.int32), zi], axis=0)
    zf = jnp.zeros((6, rt), F32)
    of_ref[...] = jnp.concatenate([g1 / gsum, g2 / gsum, zf], axis=0)


def _routing(logits_t, router_bias):
    ne, t = logits_t.shape
    rt = min(512, t)
    bias = jnp.broadcast_to(router_bias.astype(F32)[:, None], (ne, LANE))
    return pl.pallas_call(
        _route_kernel,
        grid=(t // rt,),
        in_specs=[pl.BlockSpec((ne, rt), lambda i: (0, i)), pl.BlockSpec((ne, LANE), lambda i: (0, 0))],
        out_specs=[
            pl.BlockSpec((8, rt), lambda i: (0, i)),
            pl.BlockSpec((8, rt), lambda i: (0, i)),
            pl.BlockSpec((ne, LANE), lambda i: (0, 0)),
        ],
        out_shape=[
            jax.ShapeDtypeStruct((8, t), jnp.int32),
            jax.ShapeDtypeStruct((8, t), F32),
            jax.ShapeDtypeStruct((ne, LANE), F32),
        ],
        compiler_params=_cparams(("arbitrary",)),
        name="moe_routing",
    )(logits_t, bias)


def _moe_kernel(blk_e, src_tok, dst_row, h_hbm, wg_ref, wu_ref, wd_ref, out_hbm, xbuf, ybuf, gsem, ssem):
    i = pl.program_id(0)
    nblk = pl.num_programs(0)
    rb = xbuf.shape[1]
    slot = i % 2

    def gather_start(plan_blk, s, j):
        tok = src_tok[plan_blk * rb + j]
        pltpu.make_async_copy(h_hbm.at[pl.ds(tok, 1), :], xbuf.at[s, pl.ds(j, 1), :], gsem.at[s]).start()

    def scatter_start(plan_blk, s, j):
        row = dst_row[plan_blk * rb + j]
        pltpu.make_async_copy(ybuf.at[s, pl.ds(j, 1), :], out_hbm.at[pl.ds(row, 1), :], ssem.at[s]).start()

    def gather_wait(s):
        pltpu.make_async_copy(h_hbm.at[pl.ds(0, rb), :], xbuf.at[s], gsem.at[s]).wait()

    def scatter_wait(s):
        pltpu.make_async_copy(ybuf.at[s], out_hbm.at[pl.ds(0, rb), :], ssem.at[s]).wait()

    @pl.when(i == 0)
    def _():
        ybuf[...] = jnp.zeros_like(ybuf)
        for j in range(rb):
            gather_start(1, 0, j)

    gather_wait(slot)

    @pl.when(i >= 1)
    def _():
        scatter_wait(slot)

    x = xbuf[slot].astype(BF16)
    for j in range(rb):
        gather_start(i + 2, 1 - slot, j)
        scatter_start(i, 1 - slot, j)
    hid = _silu(_dot(x, wg_ref[0])) * _dot(x, wu_ref[0])
    ybuf[slot] = _dot(hid.astype(BF16), wd_ref[0])

    @pl.when(i == nblk - 1)
    def _():
        for j in range(rb):
            scatter_start(i + 1, slot, j)
        gather_wait(1 - slot)
        scatter_wait(1 - slot)
        scatter_wait(slot)


def _experts(blk_e, src_tok, dst_row, h2, wg, wu, wd, rb):
    t, d = h2.shape
    ne, _, dff = wg.shape
    nblk = blk_e.shape[0]
    out_rows = (nblk + 1) * rb
    return pl.pallas_call(
        _moe_kernel,
        grid_spec=pltpu.PrefetchScalarGridSpec(
            num_scalar_prefetch=3,
            grid=(nblk,),
            in_specs=[
                pl.BlockSpec(memory_space=pl.ANY),
                pl.BlockSpec((1, d, dff), lambda i, be, st, dr: (be[i], 0, 0)),
                pl.BlockSpec((1, d, dff), lambda i, be, st, dr: (be[i], 0, 0)),
                pl.BlockSpec((1, dff, d), lambda i, be, st, dr: (be[i], 0, 0)),
            ],
            out_specs=pl.BlockSpec(memory_space=pl.ANY),
            scratch_shapes=[
                pltpu.VMEM((2, rb, d), F32),
                pltpu.VMEM((2, rb, d), F32),
                pltpu.SemaphoreType.DMA((2,)),
                pltpu.SemaphoreType.DMA((2,)),
            ],
        ),
        out_shape=jax.ShapeDtypeStruct((out_rows, d), F32),
        compiler_params=_cparams(("arbitrary",)),
        name="moe_experts",
    )(blk_e, src_tok, dst_row, h2, wg, wu, wd)


def _combine_kernel(x_ref, y0_ref, y1_ref, w_ref, gate_ref, g_ref, o_ref, *, final):
    w = w_ref[...]
    moe = w[:, 0:1] * y0_ref[...] + w[:, 1:2] * y1_ref[...]
    x = x_ref[...] + gate_ref[0] * moe
    if final:
        x = _rms(x, g_ref[...])
    o_ref[...] = x


def _combine(x2, y2, gates_t, mod6, final_g, seq, final):
    t, d = x2.shape
    tm = min(512, seq)
    per_b = seq // tm
    nt = t // tm
    return pl.pallas_call(
        functools.partial(_combine_kernel, final=final),
        grid=(nt,),
        in_specs=[
            pl.BlockSpec((tm, d), lambda i: (i, 0)),
            pl.BlockSpec((tm, d), lambda i: (i, 0)),
            pl.BlockSpec((tm, d), lambda i: (nt + i, 0)),
            pl.BlockSpec((tm, 2), lambda i: (i, 0)),
            pl.BlockSpec((1, 1, d), lambda i: ((i // per_b) * N_MOD + 5, 0, 0)),
            pl.BlockSpec((1, d), lambda i: (0, 0)),
        ],
        out_specs=pl.BlockSpec((tm, d), lambda i: (i, 0)),
        out_shape=jax.ShapeDtypeStruct((t, d), F32),
        compiler_params=_cparams(("arbitrary",)),
        name="moe_combine",
    )(x2, y2, y2, gates_t, mod6, final_g.reshape(1, d))


def _permute_w_in(w_in):
    depth, d, _ = w_in.shape
    gw = GROUP_W
    kv = SWA_KV_HEADS * HEAD_DIM
    xbc = gw + 2 * SSM_GROUPS * SSM_STATE
    o = 0
    a = w_in[:, :, o:o + 3 * gw]; o += 3 * gw
    bq = w_in[:, :, o:o + gw]; o += gw
    bk = w_in[:, :, o:o + kv]; o += kv
    bv = w_in[:, :, o:o + kv]; o += kv
    cz = w_in[:, :, o:o + gw]; o += gw
    cx = w_in[:, :, o:o + xbc]; o += xbc
    cdt = w_in[:, :, o:o + N_HEADS]; o += N_HEADS
    dqkv = w_in[:, :, o:o + 3 * gw]; o += 3 * gw
    zeros = lambda n: jnp.zeros((depth, d, n), w_in.dtype)
    out = jnp.concatenate([a, dqkv, cz, bq, cx, bk, bv, cdt, zeros(LANE - N_HEADS), zeros(LANE)], axis=-1)
    return out.astype(BF16)


def _dispatch_plan(idx, cnt, t, rb):
    ne = N_EXPERTS
    counts = cnt[:, 0].astype(jnp.int32)
    padded = (counts + rb - 1) // rb * rb
    pad_ends = jnp.cumsum(padded)
    pad_starts = pad_ends - padded
    e_idx = idx[0:2]
    rank = idx[2:4]
    onehot = e_idx[:, :, None] == jnp.arange(ne, dtype=jnp.int32)
    dest = jnp.sum(jnp.where(onehot, pad_starts, 0), axis=-1) + rank + rb
    nblk = (2 * t + ne * rb) // rb
    npos = (nblk + 2) * rb
    tok = jnp.broadcast_to(jnp.arange(t, dtype=jnp.int32)[None, :], (2, t))
    out_row = jnp.arange(2, dtype=jnp.int32)[:, None] * t + tok
    placed = jnp.full((npos,), -1, jnp.int32).at[dest.reshape(-1)].set(out_row.reshape(-1))
    is_pad = placed < 0
    dump_row = 2 * t + jnp.cumsum(is_pad.astype(jnp.int32)) - 1
    dst_row = jnp.where(is_pad, dump_row, placed)
    src_tok = jnp.where(is_pad, 0, jnp.where(placed >= t, placed - t, placed))
    blk_start = jnp.arange(nblk, dtype=jnp.int32) * rb
    blk_e = jnp.minimum(jnp.sum(pad_ends[None, :] <= blk_start[:, None], axis=1), ne - 1).astype(jnp.int32)
    return blk_e, src_tok, dst_row


def kernel(x, c, w_mod, b_mod, norm1_g, norm2_g, w_in, w_out, conv_a_w, gn_a, attn_sinks, gn_b, ssm_conv_w,
           ssm_conv_b, dt_bias, a_log, d_skip, ssm_norm_g, gn_d, router_w, router_bias, moe_w_gate, moe_w_up,
           moe_w_down, final_g):
    batch, seq, d = x.shape
    depth = w_in.shape[0]
    t = batch * seq
    rb = 256

    mod = _modulation(c, w_mod, b_mod)
    w_in_p = _permute_w_in(w_in)
    w_out_b = w_out.astype(BF16)
    router_w_p = jnp.zeros((d, LANE), F32).at[:, :N_EXPERTS].set(router_w)

    x2 = x.reshape(t, d)
    for l in range(depth):
        mod6 = mod[l].reshape(batch * N_MOD, 1, d)
        proj = _in_projection(x2, norm1_g[l], mod6, w_in_p[l], seq)
        ya = _mixer_a(proj, conv_a_w[l], gn_a[l], batch, seq)
        yb = _mixer_b(proj, attn_sinks[l], gn_b[l], batch, seq)
        yc = _mixer_c(proj, ssm_conv_w[l], ssm_conv_b[l], dt_bias[l], a_log[l], d_skip[l], ssm_norm_g[l],
                      batch, seq)
        yd = _mixer_d(proj, batch, seq)
        x2, h2, logits_t = _out_projection(ya, yb, yc, yd, gn_d[l], w_out_b[l], x2, mod6, norm2_g[l],
                                           router_w_p, seq)
        idx, gates, cnt = _routing(logits_t, router_bias)
        blk_e, src_tok, dst_row = _dispatch_plan(idx, cnt, t, rb)
        y2 = _experts(blk_e, src_tok, dst_row, h2, moe_w_gate[l].astype(BF16), moe_w_up[l].astype(BF16),
                      moe_w_down[l].astype(BF16), rb)
        x2 = _combine(x2, y2, gates[0:2].T, mod6, final_g, seq, final=(l == depth - 1))
    return x2.reshape(batch, seq, d)
```

```python
import functools
import math

import jax
import jax.numpy as jnp
from jax import lax
from jax.experimental import pallas as pl
from jax.experimental.pallas import tpu as pltpu

F32 = jnp.float32
BF16 = jnp.bfloat16

EPS = 1e-6
HEAD_DIM = 64
GROUP_W = 512
N_HEADS = GROUP_W // HEAD_DIM
SWA_KV_HEADS = 2
SWA_WINDOW = 128
SSM_GROUPS = 2
SSM_STATE = 128
SSM_CHUNK = 128
N_EXPERTS = 16
N_EXPERT_GROUPS = 4
EXPERTS_PER_GROUP = N_EXPERTS // N_EXPERT_GROUPS
N_MOD = 6

LANE = 128
HALO = 16
SB_KEYS = 256
ROW_CHUNKS = 16
VMEM_LIMIT = 56 * 1024 * 1024

U_A = 0
U_DQ = 12
U_DK = 16
U_DV = 20
U_Z = 24
U_BQ = 28
U_XBC = 32
U_BK = 40
U_BV = 41
U_DT = 42
N_UNITS = 44
NP = N_UNITS * LANE

PROJ_DTYPE = jnp.bfloat16


def _cparams(sem, vmem=VMEM_LIMIT):
    return pltpu.CompilerParams(dimension_semantics=sem, vmem_limit_bytes=vmem)


def _rms(y, g):
    return y * lax.rsqrt(jnp.mean(y * y, axis=-1, keepdims=True) + EPS) * g


def _split_bf16(a):
    hi = a.astype(BF16)
    lo = (a - hi.astype(F32)).astype(BF16)
    return hi, lo


def _dot(a, b):
    return jnp.dot(a, b, preferred_element_type=F32)


def _dot3(a, b):
    ah, al = _split_bf16(a)
    bh, bl = _split_bf16(b)
    return _dot(ah, bh) + _dot(al, bh) + _dot(ah, bl)


def _silu(a):
    return a * (1.0 / (1.0 + jnp.exp(-a)))


def _softplus(a):
    return jnp.maximum(a, 0.0) + jnp.log1p(jnp.exp(-jnp.abs(a)))


def _mod_kernel(c_ref, w_ref, b_ref, o_ref):
    cond = _silu(c_ref[...])
    o_ref[0] = _dot3(cond, w_ref[0]) + b_ref[0]


def _modulation(c, w_mod, b_mod):
    depth, d, nm = w_mod.shape
    b = c.shape[0]
    rows = 8
    cp = jnp.zeros((rows, d), F32).at[:b].set(c)
    tn = 1024
    out = pl.pallas_call(
        _mod_kernel,
        grid=(depth, nm // tn),
        in_specs=[
            pl.BlockSpec((rows, d), lambda l, j: (0, 0)),
            pl.BlockSpec((1, d, tn), lambda l, j: (l, 0, j)),
            pl.BlockSpec((1, 1, tn), lambda l, j: (l, 0, j)),
        ],
        out_specs=pl.BlockSpec((1, rows, tn), lambda l, j: (l, 0, j)),
        out_shape=jax.ShapeDtypeStruct((depth, rows, nm), F32),
        compiler_params=_cparams(("arbitrary", "arbitrary")),
        name="modulation",
    )(cp, w_mod, b_mod.reshape(depth, 1, nm))
    return out[:, :b]


def _inproj_kernel(x_ref, g_ref, sh_ref, sc_ref, w_ref, o_ref, h_scr):
    @pl.when(pl.program_id(1) == 0)
    def _():
        h = _rms(x_ref[...], g_ref[...]) * (1.0 + sc_ref[0]) + sh_ref[0]
        h_scr[...] = h.astype(BF16)

    o_ref[...] = _dot(h_scr[...], w_ref[...]).astype(o_ref.dtype)


def _in_projection(x2, g, mod6, w_in_p, layer, seq):
    t, d = x2.shape
    tm = min(1024, seq)
    tn = 512
    per_b = seq // tm
    return pl.pallas_call(
        _inproj_kernel,
        grid=(t // tm, NP // tn),
        in_specs=[
            pl.BlockSpec((tm, d), lambda i, j: (i, 0)),
            pl.BlockSpec((1, d), lambda i, j: (0, 0)),
            pl.BlockSpec((1, 1, d), lambda i, j: ((i // per_b) * N_MOD + 0, 0, 0)),
            pl.BlockSpec((1, 1, d), lambda i, j: ((i // per_b) * N_MOD + 1, 0, 0)),
            pl.BlockSpec((None, d, tn), lambda i, j: (layer, 0, j)),
        ],
        out_specs=pl.BlockSpec((tm, tn), lambda i, j: (i, j)),
        out_shape=jax.ShapeDtypeStruct((t, NP), PROJ_DTYPE),
        scratch_shapes=[pltpu.VMEM((tm, d), BF16)],
        compiler_params=_cparams(("arbitrary", "arbitrary")),
        name="adaln_inproj",
    )(x2, g.reshape(1, d), mod6, mod6, w_in_p)


def _mixa_kernel(p_ref, halo_ref, w_ref, g_ref, o_ref, ext_scr):
    ts = p_ref.shape[0]
    w = GROUP_W
    p = p_ref[...].astype(F32)
    hp = halo_ref[...].astype(F32)
    u = p[:, w:2 * w] * p[:, 2 * w:3 * w]
    hu = hp[:, w:2 * w] * hp[:, 2 * w:3 * w]
    hu = jnp.where(pl.program_id(1) == 0, 0.0, hu)
    ext_scr[0:HALO, :] = hu
    ext_scr[HALO:HALO + ts, :] = u
    cw = w_ref[...]
    conv = (cw[0:1] * ext_scr[HALO - 2:HALO - 2 + ts, :]
            + cw[1:2] * ext_scr[HALO - 1:HALO - 1 + ts, :]
            + cw[2:3] * u)
    o_ref[...] = _rms(p[:, 0:w] * conv, g_ref[...]).astype(o_ref.dtype)


def _mixer_a(proj, conv_w, gn, batch, seq):
    t = proj.shape[0]
    ts = min(512, seq)
    nt = seq // ts
    hb = ts // HALO
    wa = 3 * GROUP_W
    return pl.pallas_call(
        _mixa_kernel,
        grid=(batch, nt),
        in_specs=[
            pl.BlockSpec((ts, wa), lambda b, i: (b * nt + i, U_A * LANE // wa)),
            pl.BlockSpec((HALO, wa), lambda b, i: (jnp.maximum((b * nt + i) * hb - 1, 0), U_A * LANE // wa)),
            pl.BlockSpec(conv_w.shape, lambda b, i: (0, 0)),
            pl.BlockSpec((1, GROUP_W), lambda b, i: (0, 0)),
        ],
        out_specs=pl.BlockSpec((ts, GROUP_W), lambda b, i: (b * nt + i, 0)),
        out_shape=jax.ShapeDtypeStruct((t, GROUP_W), BF16),
        scratch_shapes=[pltpu.VMEM((HALO + ts, GROUP_W), F32)],
        compiler_params=_cparams(("arbitrary", "arbitrary")),
        name="mixer_a_conv",
    )(proj, proj, conv_w, gn.reshape(1, GROUP_W))


def _swa_kernel(q_ref, k_ref, kp_ref, v_ref, vp_ref, sink_ref, slope_ref, g_ref, o_ref, y_scr):
    w = SWA_WINDOW
    hd = HEAD_DIM
    grp = N_HEADS // SWA_KV_HEADS
    nblk = q_ref.shape[0] // w
    first = pl.program_id(1) == 0
    q = (q_ref[...].astype(F32) * (1.0 / math.sqrt(hd))).astype(BF16)
    k = jnp.concatenate([kp_ref[...], k_ref[...]], axis=0).astype(BF16)
    v = jnp.concatenate([vp_ref[...], v_ref[...]], axis=0).astype(BF16)
    rows = grp * w
    qi = lax.broadcasted_iota(jnp.int32, (rows, 2 * w), 0) % w
    kj = lax.broadcasted_iota(jnp.int32, (rows, 2 * w), 1)
    hrow = lax.broadcasted_iota(jnp.int32, (rows, 1), 0) // w
    dist = qi + w - kj
    valid = (dist >= 0) & (dist < w)
    distf = dist.astype(F32)
    sinks = sink_ref[...]
    slopes = slope_ref[...]

    chains = []
    for kvh in range(SWA_KV_HEADS):
        slope = jnp.zeros((rows, 1), F32)
        sink = jnp.zeros((rows, 1), F32)
        for gi in range(grp):
            h = kvh * grp + gi
            slope = jnp.where(hrow == gi, slopes[0:1, h:h + 1], slope)
            sink = jnp.where(hrow == gi, sinks[0:1, h:h + 1], sink)
        bias = jnp.where(valid, -slope * distf, -jnp.inf)
        bias0 = jnp.where(first & (kj < w), -jnp.inf, bias)
        for n in range(nblk):
            qs = jnp.concatenate([q[n * w:(n + 1) * w, (kvh * grp + gi) * hd:(kvh * grp + gi + 1) * hd]
                                  for gi in range(grp)], axis=0)
            kn = k[n * w:(n + 2) * w, kvh * hd:(kvh + 1) * hd]
            vn = v[n * w:(n + 2) * w, kvh * hd:(kvh + 1) * hd]
            chains.append((kvh, n, qs, kn, vn, bias0 if n == 0 else bias, sink))

    scores = [lax.dot_general(qs, kn, (((1,), (1,)), ((), ())), preferred_element_type=F32) + bias
              for _, _, qs, kn, _, bias, _ in chains]
    es, rden = [], []
    for (_, _, _, _, _, _, sink), s in zip(chains, scores):
        m = jnp.maximum(jnp.max(s, axis=-1, keepdims=True), sink)
        e = jnp.exp(s - m)
        rden.append(1.0 / (jnp.sum(e, axis=-1, keepdims=True) + jnp.exp(sink - m)))
        es.append(e.astype(BF16))
    outs = [_dot(e, vn) * r for (_, _, _, _, vn, _, _), e, r in zip(chains, es, rden)]
    for (kvh, n, _, _, _, _, _), o in zip(chains, outs):
        for gi in range(grp):
            h = kvh * grp + gi
            y_scr[n * w:(n + 1) * w, h * hd:(h + 1) * hd] = o[gi * w:(gi + 1) * w, :]
    o_ref[...] = _rms(y_scr[...], g_ref[...]).astype(o_ref.dtype)


def _mixer_b(proj, sinks, gn, batch, seq):
    t = proj.shape[0]
    w = SWA_WINDOW
    ts = min(512, seq)
    nt = seq // ts
    pb = ts // w
    slopes = 2.0 ** (-8.0 * jnp.arange(1, N_HEADS + 1, dtype=F32) / N_HEADS)
    pad = lambda a: jnp.zeros((1, LANE), F32).at[0, :N_HEADS].set(a)
    cur = lambda u: (lambda b, i: (b * nt + i, u))
    prev = lambda u: (lambda b, i: (jnp.maximum((b * nt + i) * pb - 1, 0), u))
    return pl.pallas_call(
        _swa_kernel,
        grid=(batch, nt),
        in_specs=[
            pl.BlockSpec((ts, GROUP_W), cur(U_BQ * LANE // GROUP_W)),
            pl.BlockSpec((ts, LANE), cur(U_BK)),
            pl.BlockSpec((w, LANE), prev(U_BK)),
            pl.BlockSpec((ts, LANE), cur(U_BV)),
            pl.BlockSpec((w, LANE), prev(U_BV)),
            pl.BlockSpec((1, LANE), lambda b, i: (0, 0)),
            pl.BlockSpec((1, LANE), lambda b, i: (0, 0)),
            pl.BlockSpec((1, GROUP_W), lambda b, i: (0, 0)),
        ],
        out_specs=pl.BlockSpec((ts, GROUP_W), lambda b, i: (b * nt + i, 0)),
        out_shape=jax.ShapeDtypeStruct((t, GROUP_W), BF16),
        scratch_shapes=[pltpu.VMEM((ts, GROUP_W), F32)],
        compiler_params=_cparams(("arbitrary", "arbitrary")),
        name="mixer_b_swa",
    )(proj, proj, proj, proj, proj, pad(sinks), pad(slopes), gn.reshape(1, GROUP_W))


def _ssd_kernel(z_ref, xbc_ref, halo_ref, dt_ref, cw_ref, cb_ref, dtb_ref, alog_ref, dsk_ref, g_ref,
                o_ref, ext_scr, y_scr, state_scr):
    l = SSM_CHUNK
    hd = HEAD_DIM
    n = SSM_STATE
    rep = N_HEADS // SSM_GROUPS
    first = pl.program_id(1) == 0

    @pl.when(first)
    def _():
        state_scr[...] = jnp.zeros_like(state_scr)

    raw = xbc_ref[...].astype(F32)
    halo = jnp.where(first, 0.0, halo_ref[...].astype(F32))
    ext_scr[0:HALO, :] = halo
    ext_scr[HALO:HALO + l, :] = raw
    cw = cw_ref[...]
    conv = (cw[0:1] * ext_scr[HALO - 3:HALO - 3 + l, :]
            + cw[1:2] * ext_scr[HALO - 2:HALO - 2 + l, :]
            + cw[2:3] * ext_scr[HALO - 1:HALO - 1 + l, :]
            + cw[3:4] * raw)
    xbc = _silu(conv + cb_ref[...])
    xs = xbc[:, 0:GROUP_W]
    bm = xbc[:, GROUP_W:GROUP_W + SSM_GROUPS * n]
    cm = xbc[:, GROUP_W + SSM_GROUPS * n:GROUP_W + 2 * SSM_GROUPS * n]

    dt = _softplus(dt_ref[...].astype(F32) + dtb_ref[...])
    da = dt * (-jnp.exp(alog_ref[...]))
    ri = lax.broadcasted_iota(jnp.int32, (l, l), 0)
    ci = lax.broadcasted_iota(jnp.int32, (l, l), 1)
    causal = ri >= ci
    tril = jnp.where(causal, 1.0, 0.0).astype(BF16)
    da_hi, da_lo = _split_bf16(da)
    cum = _dot(tril, da_hi) + _dot(tril, da_lo)
    cum_t = cum.T
    last = cum[l - 1:l, :]
    dsk = dsk_ref[...]

    for gi in range(SSM_GROUPS):
        bg = bm[:, gi * n:(gi + 1) * n]
        cg = cm[:, gi * n:(gi + 1) * n].astype(BF16)
        bg_t = bg.T.astype(BF16)
        cb = _dot(cg, bg_t)
        for hi_ in range(rep):
            h = gi * rep + hi_
            col = cum[:, h:h + 1]
            row = cum_t[h:h + 1, :]
            decay = jnp.exp(jnp.where(causal, col - row, -jnp.inf))
            xh = xs[:, h * hd:(h + 1) * hd]
            xdt = xh * dt[:, h:h + 1]
            y = _dot((cb * decay).astype(BF16), xdt.astype(BF16))
            prev = state_scr[h]
            y = y + _dot(cg, prev.astype(BF16)) * jnp.exp(col)
            y = y + xh * dsk[0:1, h:h + 1]
            y_scr[:, h * hd:(h + 1) * hd] = y
            lh = last[0:1, h:h + 1]
            to_end = jnp.exp(lh - col)
            new = _dot(bg_t, (xdt * to_end).astype(BF16))
            state_scr[h] = prev * jnp.exp(lh) + new

    y = y_scr[...] * _silu(z_ref[...].astype(F32))
    o_ref[...] = _rms(y, g_ref[...]).astype(o_ref.dtype)


def _mixer_c(proj, conv_w, conv_b, dt_bias, a_log, d_skip, norm_g, batch, seq):
    t = proj.shape[0]
    l = SSM_CHUNK
    nc = seq // l
    wx = GROUP_W + 2 * SSM_GROUPS * SSM_STATE
    hb = l // HALO
    pad = lambda a: jnp.zeros((1, LANE), F32).at[0, :N_HEADS].set(a)
    cur = lambda u: (lambda b, c: (b * nc + c, u))
    return pl.pallas_call(
        _ssd_kernel,
        grid=(batch, nc),
        in_specs=[
            pl.BlockSpec((l, GROUP_W), cur(U_Z * LANE // GROUP_W)),
            pl.BlockSpec((l, wx), cur(U_XBC * LANE // wx)),
            pl.BlockSpec((HALO, wx), lambda b, c: (jnp.maximum((b * nc + c) * hb - 1, 0), U_XBC * LANE // wx)),
            pl.BlockSpec((l, LANE), cur(U_DT)),
            pl.BlockSpec(conv_w.shape, lambda b, c: (0, 0)),
            pl.BlockSpec((1, wx), lambda b, c: (0, 0)),
            pl.BlockSpec((1, LANE), lambda b, c: (0, 0)),
            pl.BlockSpec((1, LANE), lambda b, c: (0, 0)),
            pl.BlockSpec((1, LANE), lambda b, c: (0, 0)),
            pl.BlockSpec((1, GROUP_W), lambda b, c: (0, 0)),
        ],
        out_specs=pl.BlockSpec((l, GROUP_W), lambda b, c: (b * nc + c, 0)),
        out_shape=jax.ShapeDtypeStruct((t, GROUP_W), BF16),
        scratch_shapes=[
            pltpu.VMEM((HALO + l, wx), F32),
            pltpu.VMEM((l, GROUP_W), F32),
            pltpu.VMEM((N_HEADS, SSM_STATE, HEAD_DIM), F32),
        ],
        compiler_params=_cparams(("arbitrary", "arbitrary")),
        name="mixer_c_ssd",
    )(proj, proj, proj, proj, conv_w, conv_b.reshape(1, wx), pad(dt_bias), pad(a_log), pad(d_skip),
      norm_g.reshape(1, GROUP_W))


def _sb_kernel(q_ref, k_ref, v_ref, o_ref):
    kb_rows = SB_KEYS
    tq = q_ref.shape[0]
    nr = tq // kb_rows
    hd = HEAD_DIM
    nh = LANE // hd
    qi = pl.program_id(2)
    ri = lax.broadcasted_iota(jnp.int32, (kb_rows, kb_rows), 0)
    ci = lax.broadcasted_iota(jnp.int32, (kb_rows, kb_rows), 1)
    strict = ci < ri
    later = jnp.where(ri > ci, 1.0, 0.0).astype(BF16)
    lanes = [slice(h * hd, (h + 1) * hd) for h in range(nh)]
    qscale = math.log2(math.e) / math.sqrt(hd)
    qs = [[(q_ref[r * kb_rows:(r + 1) * kb_rows, ln].astype(F32) * qscale).astype(BF16) for r in range(nr)]
          for ln in lanes]
    sign = jnp.uint32(0x80000000)

    def stages(chains):
        zs = [lax.dot_general(q, k, (((1,), (1,)), ((), ())), preferred_element_type=F32)
              for q, k, _, _, _, _ in chains]
        sps = []
        for (_, _, _, keep, _, _), z in zip(chains, zs):
            neg_abs = lax.bitcast_convert_type(lax.bitcast_convert_type(z, jnp.uint32) | sign, F32)
            sp = jnp.maximum(z, 0.0) + jnp.log(1.0 + jnp.exp2(neg_abs)) * math.log2(math.e)
            sps.append(sp if keep is None else jnp.where(keep, sp, 0.0))
        afters = [_dot(sp.astype(BF16), later) for sp in sps]
        probs = []
        for (_, _, _, keep, _, _), z, sp, after in zip(chains, zs, sps, afters):
            a = jnp.exp2(z - sp - after)
            probs.append((a if keep is None else jnp.where(keep, a, 0.0)).astype(BF16))
        outs = [_dot(a, v) for (_, _, v, _, _, _), a in zip(chains, probs)]
        return [(run + jnp.sum(sp, axis=-1, keepdims=True), acc + jnp.exp2(-run) * o)
                for (_, _, _, _, run, acc), sp, o in zip(chains, sps, outs)]

    def step(kb, carry, kinds):
        rows = pl.ds(pl.multiple_of(kb * kb_rows, kb_rows), kb_rows)
        chains, where = [], []
        for h in range(nh):
            k = k_ref[rows, lanes[h]].astype(BF16)
            v = v_ref[rows, lanes[h]].astype(BF16)
            for r in range(nr):
                if kinds[r] is not None:
                    run, acc = carry[h * nr + r]
                    chains.append((qs[h][r], k, v, strict if kinds[r] == "diag" else None, run, acc))
                    where.append(h * nr + r)
        carry = list(carry)
        for i, rc in zip(where, stages(chains)):
            carry[i] = rc
        return tuple(carry)

    carry = tuple((jnp.zeros((kb_rows, 1), F32), jnp.zeros((kb_rows, hd), F32)) for _ in range(nh * nr))
    for d in reversed(range(nr)):
        kinds = [None if r < d else ("diag" if r == d else "full") for r in range(nr)]
        carry = step(qi * nr + d, carry, kinds)
    nfull = qi * nr
    carry = lax.fori_loop(0, nfull, lambda j, c: step(nfull - 1 - j, c, ["full"] * nr), carry)
    for h in range(nh):
        for r in range(nr):
            o_ref[r * kb_rows:(r + 1) * kb_rows, lanes[h]] = carry[h * nr + r][1].astype(o_ref.dtype)


def _mixer_d(proj, batch, seq):
    t = proj.shape[0]
    tq = min(2 * SB_KEYS, seq)
    nq = seq // tq
    pairs = GROUP_W // LANE
    return pl.pallas_call(
        _sb_kernel,
        grid=(batch, pairs, nq),
        in_specs=[
            pl.BlockSpec((tq, LANE), lambda b, p, i: (b * nq + i, U_DQ + p)),
            pl.BlockSpec((seq, LANE), lambda b, p, i: (b, U_DK + p)),
            pl.BlockSpec((seq, LANE), lambda b, p, i: (b, U_DV + p)),
        ],
        out_specs=pl.BlockSpec((tq, LANE), lambda b, p, i: (b * nq + i, p)),
        out_shape=jax.ShapeDtypeStruct((t, GROUP_W), BF16),
        compiler_params=_cparams(("arbitrary", "arbitrary", "arbitrary")),
        name="mixer_d_stickbreaking",
    )(proj, proj, proj)


def _outproj_kernel(ya_ref, yb_ref, yc_ref, yd_ref, gnd_ref, w_ref, x_ref, gate_ref, sh_ref, sc_ref, g_ref,
                    rw_ref, xo_ref, h_ref, lg_ref):
    w = GROUP_W
    yd = _rms(yd_ref[...].astype(F32), gnd_ref[...]).astype(BF16)
    mix = (_dot(ya_ref[...], w_ref[0:w, :]) + _dot(yb_ref[...], w_ref[w:2 * w, :])
           + _dot(yc_ref[...], w_ref[2 * w:3 * w, :]) + _dot(yd, w_ref[3 * w:4 * w, :]))
    x = x_ref[...] + gate_ref[0] * mix
    xo_ref[...] = x
    h = _rms(x, g_ref[...]) * (1.0 + sc_ref[0]) + sh_ref[0]
    for c in range(ROW_CHUNKS):
        h_ref[pl.ds(c, h.shape[0], stride=ROW_CHUNKS), :] = h[:, c * LANE:(c + 1) * LANE]
    lg_ref[...] = _dot3(h, rw_ref[...]).T[0:N_EXPERTS, :]


def _out_projection(ya, yb, yc, yd, gn_d, w_out_b, layer, x2, mod6, g2, router_w_p, seq):
    t, d = x2.shape
    tm = min(256, seq)
    per_b = seq // tm
    row = lambda i: (i, 0)
    const = lambda i: (0, 0)
    modspec = lambda m: pl.BlockSpec((1, 1, d), lambda i: ((i // per_b) * N_MOD + m, 0, 0))
    return pl.pallas_call(
        _outproj_kernel,
        grid=(t // tm,),
        in_specs=[
            pl.BlockSpec((tm, GROUP_W), row), pl.BlockSpec((tm, GROUP_W), row),
            pl.BlockSpec((tm, GROUP_W), row), pl.BlockSpec((tm, GROUP_W), row),
            pl.BlockSpec((1, GROUP_W), const),
            pl.BlockSpec((None, d, d), lambda i: (layer, 0, 0)),
            pl.BlockSpec((tm, d), row),
            modspec(2), modspec(3), modspec(4),
            pl.BlockSpec((1, d), const),
            pl.BlockSpec((d, LANE), const),
        ],
        out_specs=[
            pl.BlockSpec((tm, d), row),
            pl.BlockSpec((tm * ROW_CHUNKS, LANE), row),
            pl.BlockSpec((N_EXPERTS, tm), lambda i: (0, i)),
        ],
        out_shape=[
            jax.ShapeDtypeStruct((t, d), F32),
            jax.ShapeDtypeStruct((t * ROW_CHUNKS, LANE), F32),
            jax.ShapeDtypeStruct((N_EXPERTS, t), F32),
        ],
        compiler_params=_cparams(("arbitrary",)),
        name="outproj_adaln_router",
    )(ya, yb, yc, yd, gn_d.reshape(1, GROUP_W), w_out_b, x2, mod6, mod6, mod6, g2.reshape(1, d), router_w_p)


def _route_kernel(lg_ref, bias_ref, oi_ref, of_ref, cnt_ref):
    ne = N_EXPERTS
    rt = lg_ref.shape[1]

    @pl.when(pl.program_id(0) == 0)
    def _():
        cnt_ref[...] = jnp.zeros_like(cnt_ref)

    lg = lg_ref[...]
    e = jnp.exp(lg - jnp.max(lg, axis=0, keepdims=True))
    probs = e / jnp.sum(e, axis=0, keepdims=True)
    sel = probs + bias_ref[...][:, 0:1]
    rowi = lax.broadcasted_iota(jnp.int32, (ne, rt), 0)
    neg = -jnp.inf

    gscore = []
    for g in range(N_EXPERT_GROUPS):
        r = [sel[g * EXPERTS_PER_GROUP + i:g * EXPERTS_PER_GROUP + i + 1, :] for i in range(EXPERTS_PER_GROUP)]
        best = None
        for i in range(EXPERTS_PER_GROUP):
            for j in range(i + 1, EXPERTS_PER_GROUP):
                pair = jnp.maximum(r[i], r[j]) + jnp.minimum(r[i], r[j])
                best = pair if best is None else jnp.maximum(best, pair)
        gscore.append(best)
    top_g = jnp.zeros((1, rt), jnp.int32)
    top_s = gscore[0]
    for g in range(1, N_EXPERT_GROUPS):
        better = gscore[g] > top_s
        top_g = jnp.where(better, g, top_g)
        top_s = jnp.where(better, gscore[g], top_s)

    in_grp = (rowi // EXPERTS_PER_GROUP) == top_g
    masked = jnp.where(in_grp, sel, neg)
    m1 = jnp.max(masked, axis=0, keepdims=True)
    i1 = jnp.min(jnp.where(masked == m1, rowi, ne), axis=0, keepdims=True)
    masked2 = jnp.where(rowi == i1, neg, masked)
    m2 = jnp.max(masked2, axis=0, keepdims=True)
    i2 = jnp.min(jnp.where(masked2 == m2, rowi, ne), axis=0, keepdims=True)
    hit1 = rowi == i1
    hit2 = rowi == i2
    g1 = jnp.sum(jnp.where(hit1, probs, 0.0), axis=0, keepdims=True)
    g2 = jnp.sum(jnp.where(hit2, probs, 0.0), axis=0, keepdims=True)
    gsum = g1 + g2

    onehot = jnp.where(hit1 | hit2, 1.0, 0.0)
    ji = lax.broadcasted_iota(jnp.int32, (rt, rt), 0)
    ti = lax.broadcasted_iota(jnp.int32, (rt, rt), 1)
    before = jnp.where(ji < ti, 1.0, 0.0).astype(BF16)
    base = cnt_ref[...][:, 0:1]
    prior = _dot(onehot.astype(BF16), before) + base
    r1 = jnp.sum(jnp.where(hit1, prior, 0.0), axis=0, keepdims=True)
    r2 = jnp.sum(jnp.where(hit2, prior, 0.0), axis=0, keepdims=True)
    cnt_ref[...] = cnt_ref[...] + jnp.sum(onehot, axis=1, keepdims=True)

    zi = jnp.zeros((4, rt), jnp.int32)
    oi_ref[...] = jnp.concatenate([i1, i2, r1.astype(jnp.int32), r2.astype(jnp.int32), zi], axis=0)
    zf = jnp.zeros((6, rt), F32)
    of_ref[...] = jnp.concatenate([g1 / gsum, g2 / gsum, zf], axis=0)


def _routing(logits_t, router_bias):
    ne, t = logits_t.shape
    rt = min(512, t)
    bias = jnp.broadcast_to(router_bias.astype(F32)[:, None], (ne, LANE))
    return pl.pallas_call(
        _route_kernel,
        grid=(t // rt,),
        in_specs=[pl.BlockSpec((ne, rt), lambda i: (0, i)), pl.BlockSpec((ne, LANE), lambda i: (0, 0))],
        out_specs=[
            pl.BlockSpec((8, rt), lambda i: (0, i)),
            pl.BlockSpec((8, rt), lambda i: (0, i)),
            pl.BlockSpec((ne, LANE), lambda i: (0, 0)),
        ],
        out_shape=[
            jax.ShapeDtypeStruct((8, t), jnp.int32),
            jax.ShapeDtypeStruct((8, t), F32),
            jax.ShapeDtypeStruct((ne, LANE), F32),
        ],
        compiler_params=_cparams(("arbitrary",)),
        name="moe_routing",
    )(logits_t, bias)


def _moe_kernel(blk_e, src_tok, dst_row, h_hbm, wg_ref, wu_ref, wd_ref, out_hbm, xbuf, ybuf, gsem, ssem):
    i = pl.program_id(0)
    nblk = pl.num_programs(0)
    rc = ROW_CHUNKS
    rb = xbuf.shape[1] // rc
    slot = i % 2

    def gather_start(plan_blk, s, j):
        tok = pl.multiple_of(src_tok[plan_blk * rb + j] * rc, rc)
        pltpu.make_async_copy(h_hbm.at[pl.ds(tok, rc), :], xbuf.at[s, pl.ds(j * rc, rc), :], gsem.at[s]).start()

    def scatter_start(plan_blk, s, j):
        row = pl.multiple_of(dst_row[plan_blk * rb + j] * rc, rc)
        pltpu.make_async_copy(ybuf.at[s, pl.ds(j * rc, rc), :], out_hbm.at[pl.ds(row, rc), :], ssem.at[s]).start()

    def gather_wait(s):
        pltpu.make_async_copy(h_hbm.at[pl.ds(0, rb * rc), :], xbuf.at[s], gsem.at[s]).wait()

    def scatter_wait(s):
        pltpu.make_async_copy(ybuf.at[s], out_hbm.at[pl.ds(0, rb * rc), :], ssem.at[s]).wait()

    @pl.when(i == 0)
    def _():
        ybuf[...] = jnp.zeros_like(ybuf)
        for j in range(rb):
            gather_start(1, 0, j)

    gather_wait(slot)

    @pl.when(i >= 1)
    def _():
        scatter_wait(slot)

    x = jnp.concatenate([xbuf[slot, pl.ds(c, rb, stride=rc), :] for c in range(rc)], axis=1).astype(BF16)
    for j in range(rb):
        gather_start(i + 2, 1 - slot, j)
        scatter_start(i, 1 - slot, j)
    hid = _silu(_dot(x, wg_ref[0])) * _dot(x, wu_ref[0])
    y = _dot(hid.astype(BF16), wd_ref[0])
    for c in range(rc):
        ybuf[slot, pl.ds(c, rb, stride=rc), :] = y[:, c * LANE:(c + 1) * LANE]

    @pl.when(i == nblk - 1)
    def _():
        for j in range(rb):
            scatter_start(i + 1, slot, j)
        gather_wait(1 - slot)
        scatter_wait(1 - slot)
        scatter_wait(slot)


def _experts(blk_e, src_tok, dst_row, h2, wg, wu, wd, layer, rb):
    rc = ROW_CHUNKS
    _, ne, d, dff = wg.shape
    nblk = blk_e.shape[0]
    out_rows = (nblk + 1) * rb
    return pl.pallas_call(
        _moe_kernel,
        grid_spec=pltpu.PrefetchScalarGridSpec(
            num_scalar_prefetch=3,
            grid=(nblk,),
            in_specs=[
                pl.BlockSpec(memory_space=pl.ANY),
                pl.BlockSpec((None, 1, d, dff), lambda i, be, st, dr: (layer, be[i], 0, 0)),
                pl.BlockSpec((None, 1, d, dff), lambda i, be, st, dr: (layer, be[i], 0, 0)),
                pl.BlockSpec((None, 1, dff, d), lambda i, be, st, dr: (layer, be[i], 0, 0)),
            ],
            out_specs=pl.BlockSpec(memory_space=pl.ANY),
            scratch_shapes=[
                pltpu.VMEM((2, rb * rc, LANE), F32),
                pltpu.VMEM((2, rb * rc, LANE), F32),
                pltpu.SemaphoreType.DMA((2,)),
                pltpu.SemaphoreType.DMA((2,)),
            ],
        ),
        out_shape=jax.ShapeDtypeStruct((out_rows * rc, LANE), F32),
        compiler_params=_cparams(("arbitrary",)),
        name="moe_experts",
    )(blk_e, src_tok, dst_row, h2, wg, wu, wd)


def _combine_kernel(x_ref, y0_ref, y1_ref, w_ref, gate_ref, g_ref, o_ref, *, final):
    w = w_ref[...]
    tm = x_ref.shape[0]
    rows = lambda ref: jnp.concatenate([ref[pl.ds(c, tm, stride=ROW_CHUNKS), :] for c in range(ROW_CHUNKS)], axis=1)
    moe = w[:, 0:1] * rows(y0_ref) + w[:, 1:2] * rows(y1_ref)
    x = x_ref[...] + gate_ref[0] * moe
    if final:
        x = _rms(x, g_ref[...])
    o_ref[...] = x


def _combine(x2, y2, gates_t, mod6, final_g, seq, final):
    t, d = x2.shape
    tm = min(512, seq)
    per_b = seq // tm
    nt = t // tm
    return pl.pallas_call(
        functools.partial(_combine_kernel, final=final),
        grid=(nt,),
        in_specs=[
            pl.BlockSpec((tm, d), lambda i: (i, 0)),
            pl.BlockSpec((tm * ROW_CHUNKS, LANE), lambda i: (i, 0)),
            pl.BlockSpec((tm * ROW_CHUNKS, LANE), lambda i: (nt + i, 0)),
            pl.BlockSpec((tm, 2), lambda i: (i, 0)),
            pl.BlockSpec((1, 1, d), lambda i: ((i // per_b) * N_MOD + 5, 0, 0)),
            pl.BlockSpec((1, d), lambda i: (0, 0)),
        ],
        out_specs=pl.BlockSpec((tm, d), lambda i: (i, 0)),
        out_shape=jax.ShapeDtypeStruct((t, d), F32),
        compiler_params=_cparams(("arbitrary",)),
        name="moe_combine",
    )(x2, y2, y2, gates_t, mod6, final_g.reshape(1, d))


def _permute_w_in(w_in):
    depth, d, _ = w_in.shape
    gw = GROUP_W
    kv = SWA_KV_HEADS * HEAD_DIM
    xbc = gw + 2 * SSM_GROUPS * SSM_STATE
    o = 0
    a = w_in[:, :, o:o + 3 * gw]; o += 3 * gw
    bq = w_in[:, :, o:o + gw]; o += gw
    bk = w_in[:, :, o:o + kv]; o += kv
    bv = w_in[:, :, o:o + kv]; o += kv
    cz = w_in[:, :, o:o + gw]; o += gw
    cx = w_in[:, :, o:o + xbc]; o += xbc
    cdt = w_in[:, :, o:o + N_HEADS]; o += N_HEADS
    dqkv = w_in[:, :, o:o + 3 * gw]; o += 3 * gw
    zeros = lambda n: jnp.zeros((depth, d, n), w_in.dtype)
    out = jnp.concatenate([a, dqkv, cz, bq, cx, bk, bv, cdt, zeros(LANE - N_HEADS), zeros(LANE)], axis=-1)
    return out.astype(BF16)


def _dispatch_plan(idx, cnt, t, rb):
    ne = N_EXPERTS
    counts = cnt[:, 0].astype(jnp.int32)
    padded = (counts + rb - 1) // rb * rb
    pad_ends = jnp.cumsum(padded)
    pad_starts = pad_ends - padded
    e_idx = idx[0:2]
    rank = idx[2:4]
    onehot = e_idx[:, :, None] == jnp.arange(ne, dtype=jnp.int32)
    dest = jnp.sum(jnp.where(onehot, pad_starts, 0), axis=-1) + rank + rb
    nblk = (2 * t + ne * rb) // rb
    npos = (nblk + 2) * rb
    tok = jnp.broadcast_to(jnp.arange(t, dtype=jnp.int32)[None, :], (2, t))
    out_row = jnp.arange(2, dtype=jnp.int32)[:, None] * t + tok
    placed = jnp.full((npos,), -1, jnp.int32).at[dest.reshape(-1)].set(out_row.reshape(-1))
    is_pad = placed < 0
    dump_row = 2 * t + jnp.cumsum(is_pad.astype(jnp.int32)) - 1
    dst_row = jnp.where(is_pad, dump_row, placed)
    src_tok = jnp.where(is_pad, 0, jnp.where(placed >= t, placed - t, placed))
    blk_start = jnp.arange(nblk, dtype=jnp.int32) * rb
    blk_e = jnp.minimum(jnp.sum(pad_ends[None, :] <= blk_start[:, None], axis=1), ne - 1).astype(jnp.int32)
    return blk_e, src_tok, dst_row


def kernel(x, c, w_mod, b_mod, norm1_g, norm2_g, w_in, w_out, conv_a_w, gn_a, attn_sinks, gn_b, ssm_conv_w,
           ssm_conv_b, dt_bias, a_log, d_skip, ssm_norm_g, gn_d, router_w, router_bias, moe_w_gate, moe_w_up,
           moe_w_down, final_g):
    batch, seq, d = x.shape
    assert d == ROW_CHUNKS * LANE
    depth = w_in.shape[0]
    t = batch * seq
    rb = 256

    mod = _modulation(c, w_mod, b_mod)
    w_in_p = _permute_w_in(w_in)
    w_out_b = w_out.astype(BF16)
    wg_b, wu_b, wd_b = moe_w_gate.astype(BF16), moe_w_up.astype(BF16), moe_w_down.astype(BF16)
    router_w_p = jnp.zeros((d, LANE), F32).at[:, :N_EXPERTS].set(router_w)

    x2 = x.reshape(t, d)
    for l in range(depth):
        mod6 = mod[l].reshape(batch * N_MOD, 1, d)
        proj = _in_projection(x2, norm1_g[l], mod6, w_in_p, l, seq)
        ya = _mixer_a(proj, conv_a_w[l], gn_a[l], batch, seq)
        yb = _mixer_b(proj, attn_sinks[l], gn_b[l], batch, seq)
        yc = _mixer_c(proj, ssm_conv_w[l], ssm_conv_b[l], dt_bias[l], a_log[l], d_skip[l], ssm_norm_g[l],
                      batch, seq)
        yd = _mixer_d(proj, batch, seq)
        x2, h2, logits_t = _out_projection(ya, yb, yc, yd, gn_d[l], w_out_b, l, x2, mod6, norm2_g[l],
                                           router_w_p, seq)
        idx, gates, cnt = _routing(logits_t, router_bias)
        blk_e, src_tok, dst_row = _dispatch_plan(idx, cnt, t, rb)
        y2 = _experts(blk_e, src_tok, dst_row, h2, wg_b, wu_b, wd_b, l, rb)
        x2 = _combine(x2, y2, gates[0:2].T, mod6, final_g, seq, final=(l == depth - 1))
    return x2.reshape(batch, seq, d)
```

```python
import functools
import math

import jax
import jax.numpy as jnp
from jax import lax
from jax.experimental import pallas as pl
from jax.experimental.pallas import tpu as pltpu

F32 = jnp.float32
BF16 = jnp.bfloat16

EPS = 1e-6
HEAD_DIM = 64
GROUP_W = 512
N_HEADS = GROUP_W // HEAD_DIM
SWA_KV_HEADS = 2
SWA_WINDOW = 128
SSM_GROUPS = 2
SSM_STATE = 128
SSM_CHUNK = 128
N_EXPERTS = 16
N_EXPERT_GROUPS = 4
EXPERTS_PER_GROUP = N_EXPERTS // N_EXPERT_GROUPS
N_MOD = 6

LANE = 128
HALO = 16
SB_KEYS = 256
ROW_CHUNKS = 16
MOE_SLOTS = 3
VMEM_LIMIT = 56 * 1024 * 1024

U_A = 0
U_DQ = 12
U_DK = 16
U_DV = 20
U_Z = 24
U_BQ = 28
U_XBC = 32
U_BK = 40
U_BV = 41
U_DT = 42
N_UNITS = 44
NP = N_UNITS * LANE

PROJ_DTYPE = jnp.bfloat16


def _cparams(sem, vmem=VMEM_LIMIT):
    return pltpu.CompilerParams(dimension_semantics=sem, vmem_limit_bytes=vmem)


def _rms(y, g):
    return y * lax.rsqrt(jnp.mean(y * y, axis=-1, keepdims=True) + EPS) * g


def _split_bf16(a):
    hi = a.astype(BF16)
    lo = (a - hi.astype(F32)).astype(BF16)
    return hi, lo


def _dot(a, b):
    return jnp.dot(a, b, preferred_element_type=F32)


def _dot3(a, b):
    ah, al = _split_bf16(a)
    bh, bl = _split_bf16(b)
    return _dot(ah, bh) + _dot(al, bh) + _dot(ah, bl)


def _silu(a):
    return a * (1.0 / (1.0 + jnp.exp(-a)))


def _softplus(a):
    return jnp.maximum(a, 0.0) + jnp.log1p(jnp.exp(-jnp.abs(a)))


def _mod_kernel(c_ref, w_ref, b_ref, o_ref):
    cond = _silu(c_ref[...])
    o_ref[0] = _dot3(cond, w_ref[0]) + b_ref[0]


def _modulation(c, w_mod, b_mod):
    depth, d, nm = w_mod.shape
    b = c.shape[0]
    rows = 8
    cp = jnp.zeros((rows, d), F32).at[:b].set(c)
    tn = 1024
    out = pl.pallas_call(
        _mod_kernel,
        grid=(depth, nm // tn),
        in_specs=[
            pl.BlockSpec((rows, d), lambda l, j: (0, 0)),
            pl.BlockSpec((1, d, tn), lambda l, j: (l, 0, j)),
            pl.BlockSpec((1, 1, tn), lambda l, j: (l, 0, j)),
        ],
        out_specs=pl.BlockSpec((1, rows, tn), lambda l, j: (l, 0, j)),
        out_shape=jax.ShapeDtypeStruct((depth, rows, nm), F32),
        compiler_params=_cparams(("arbitrary", "arbitrary")),
        name="modulation",
    )(cp, w_mod, b_mod.reshape(depth, 1, nm))
    return out[:, :b]


def _inproj_kernel(x_ref, g_ref, sh_ref, sc_ref, w_ref, o_ref, h_scr):
    @pl.when(pl.program_id(1) == 0)
    def _():
        chunk = LANE

        def body(r, carry):
            rows = pl.ds(pl.multiple_of(r * chunk, chunk), chunk)
            h = _rms(x_ref[rows, :], g_ref[...]) * (1.0 + sc_ref[0]) + sh_ref[0]
            h_scr[rows, :] = h.astype(BF16)
            return carry

        lax.fori_loop(0, x_ref.shape[0] // chunk, body, 0)

    o_ref[...] = _dot(h_scr[...], w_ref[...]).astype(o_ref.dtype)


def _in_projection(x2, g, mod6, w_in_p, layer, seq):
    t, d = x2.shape
    tm = min(1024, seq)
    tn = 512
    per_b = seq // tm
    return pl.pallas_call(
        _inproj_kernel,
        grid=(t // tm, NP // tn),
        in_specs=[
            pl.BlockSpec((tm, d), lambda i, j: (i, 0)),
            pl.BlockSpec((1, d), lambda i, j: (0, 0)),
            pl.BlockSpec((1, 1, d), lambda i, j: ((i // per_b) * N_MOD + 0, 0, 0)),
            pl.BlockSpec((1, 1, d), lambda i, j: ((i // per_b) * N_MOD + 1, 0, 0)),
            pl.BlockSpec((None, d, tn), lambda i, j: (layer, 0, j)),
        ],
        out_specs=pl.BlockSpec((tm, tn), lambda i, j: (i, j)),
        out_shape=jax.ShapeDtypeStruct((t, NP), PROJ_DTYPE),
        scratch_shapes=[pltpu.VMEM((tm, d), BF16)],
        compiler_params=_cparams(("arbitrary", "arbitrary")),
        name="adaln_inproj",
    )(x2, g.reshape(1, d), mod6, mod6, w_in_p)


def _mixa_kernel(p_ref, halo_ref, w_ref, g_ref, o_ref, ext_scr):
    ts = p_ref.shape[0]
    w = GROUP_W
    p = p_ref[...].astype(F32)
    hp = halo_ref[...].astype(F32)
    u = p[:, w:2 * w] * p[:, 2 * w:3 * w]
    hu = hp[:, w:2 * w] * hp[:, 2 * w:3 * w]
    hu = jnp.where(pl.program_id(1) == 0, 0.0, hu)
    ext_scr[0:HALO, :] = hu
    ext_scr[HALO:HALO + ts, :] = u
    cw = w_ref[...]
    conv = (cw[0:1] * ext_scr[HALO - 2:HALO - 2 + ts, :]
            + cw[1:2] * ext_scr[HALO - 1:HALO - 1 + ts, :]
            + cw[2:3] * u)
    o_ref[...] = _rms(p[:, 0:w] * conv, g_ref[...]).astype(o_ref.dtype)


def _mixer_a(proj, conv_w, gn, batch, seq):
    t = proj.shape[0]
    ts = min(512, seq)
    nt = seq // ts
    hb = ts // HALO
    wa = 3 * GROUP_W
    return pl.pallas_call(
        _mixa_kernel,
        grid=(batch, nt),
        in_specs=[
            pl.BlockSpec((ts, wa), lambda b, i: (b * nt + i, U_A * LANE // wa)),
            pl.BlockSpec((HALO, wa), lambda b, i: (jnp.maximum((b * nt + i) * hb - 1, 0), U_A * LANE // wa)),
            pl.BlockSpec(conv_w.shape, lambda b, i: (0, 0)),
            pl.BlockSpec((1, GROUP_W), lambda b, i: (0, 0)),
        ],
        out_specs=pl.BlockSpec((ts, GROUP_W), lambda b, i: (b * nt + i, 0)),
        out_shape=jax.ShapeDtypeStruct((t, GROUP_W), BF16),
        scratch_shapes=[pltpu.VMEM((HALO + ts, GROUP_W), F32)],
        compiler_params=_cparams(("arbitrary", "arbitrary")),
        name="mixer_a_conv",
    )(proj, proj, conv_w, gn.reshape(1, GROUP_W))


def _swa_kernel(q_ref, k_ref, kp_ref, v_ref, vp_ref, sink_ref, slope_ref, g_ref, o_ref, y_scr):
    w = SWA_WINDOW
    hd = HEAD_DIM
    grp = N_HEADS // SWA_KV_HEADS
    nblk = q_ref.shape[0] // w
    first = pl.program_id(1) == 0
    q = (q_ref[...].astype(F32) * (1.0 / math.sqrt(hd))).astype(BF16)
    k = jnp.concatenate([kp_ref[...], k_ref[...]], axis=0).astype(BF16)
    v = jnp.concatenate([vp_ref[...], v_ref[...]], axis=0).astype(BF16)
    rows = grp * w
    qi = lax.broadcasted_iota(jnp.int32, (rows, 2 * w), 0) % w
    kj = lax.broadcasted_iota(jnp.int32, (rows, 2 * w), 1)
    hrow = lax.broadcasted_iota(jnp.int32, (rows, 1), 0) // w
    dist = qi + w - kj
    valid = (dist >= 0) & (dist < w)
    distf = dist.astype(F32)
    sinks = sink_ref[...]
    slopes = slope_ref[...]

    chains = []
    for kvh in range(SWA_KV_HEADS):
        slope = jnp.zeros((rows, 1), F32)
        sink = jnp.zeros((rows, 1), F32)
        for gi in range(grp):
            h = kvh * grp + gi
            slope = jnp.where(hrow == gi, slopes[0:1, h:h + 1], slope)
            sink = jnp.where(hrow == gi, sinks[0:1, h:h + 1], sink)
        bias = jnp.where(valid, -slope * distf, -jnp.inf)
        bias0 = jnp.where(first & (kj < w), -jnp.inf, bias)
        for n in range(nblk):
            qs = jnp.concatenate([q[n * w:(n + 1) * w, (kvh * grp + gi) * hd:(kvh * grp + gi + 1) * hd]
                                  for gi in range(grp)], axis=0)
            kn = k[n * w:(n + 2) * w, kvh * hd:(kvh + 1) * hd]
            vn = v[n * w:(n + 2) * w, kvh * hd:(kvh + 1) * hd]
            chains.append((kvh, n, qs, kn, vn, bias0 if n == 0 else bias, sink))

    scores = [lax.dot_general(qs, kn, (((1,), (1,)), ((), ())), preferred_element_type=F32) + bias
              for _, _, qs, kn, _, bias, _ in chains]
    es, rden = [], []
    for (_, _, _, _, _, _, sink), s in zip(chains, scores):
        m = jnp.maximum(jnp.max(s, axis=-1, keepdims=True), sink)
        e = jnp.exp(s - m)
        rden.append(1.0 / (jnp.sum(e, axis=-1, keepdims=True) + jnp.exp(sink - m)))
        es.append(e.astype(BF16))
    outs = [_dot(e, vn) * r for (_, _, _, _, vn, _, _), e, r in zip(chains, es, rden)]
    for (kvh, n, _, _, _, _, _), o in zip(chains, outs):
        for gi in range(grp):
            h = kvh * grp + gi
            y_scr[n * w:(n + 1) * w, h * hd:(h + 1) * hd] = o[gi * w:(gi + 1) * w, :]
    o_ref[...] = _rms(y_scr[...], g_ref[...]).astype(o_ref.dtype)


def _mixer_b(proj, sinks, gn, batch, seq):
    t = proj.shape[0]
    w = SWA_WINDOW
    ts = min(512, seq)
    nt = seq // ts
    pb = ts // w
    slopes = 2.0 ** (-8.0 * jnp.arange(1, N_HEADS + 1, dtype=F32) / N_HEADS)
    pad = lambda a: jnp.zeros((1, LANE), F32).at[0, :N_HEADS].set(a)
    cur = lambda u: (lambda b, i: (b * nt + i, u))
    prev = lambda u: (lambda b, i: (jnp.maximum((b * nt + i) * pb - 1, 0), u))
    return pl.pallas_call(
        _swa_kernel,
        grid=(batch, nt),
        in_specs=[
            pl.BlockSpec((ts, GROUP_W), cur(U_BQ * LANE // GROUP_W)),
            pl.BlockSpec((ts, LANE), cur(U_BK)),
            pl.BlockSpec((w, LANE), prev(U_BK)),
            pl.BlockSpec((ts, LANE), cur(U_BV)),
            pl.BlockSpec((w, LANE), prev(U_BV)),
            pl.BlockSpec((1, LANE), lambda b, i: (0, 0)),
            pl.BlockSpec((1, LANE), lambda b, i: (0, 0)),
            pl.BlockSpec((1, GROUP_W), lambda b, i: (0, 0)),
        ],
        out_specs=pl.BlockSpec((ts, GROUP_W), lambda b, i: (b * nt + i, 0)),
        out_shape=jax.ShapeDtypeStruct((t, GROUP_W), BF16),
        scratch_shapes=[pltpu.VMEM((ts, GROUP_W), F32)],
        compiler_params=_cparams(("arbitrary", "arbitrary")),
        name="mixer_b_swa",
    )(proj, proj, proj, proj, proj, pad(sinks), pad(slopes), gn.reshape(1, GROUP_W))


def _ssd_kernel(z_ref, xbc_ref, halo_ref, dt_ref, cw_ref, cb_ref, dtb_ref, alog_ref, dsk_ref, g_ref,
                o_ref, ext_scr, y_scr, state_scr):
    l = SSM_CHUNK
    hd = HEAD_DIM
    n = SSM_STATE
    rep = N_HEADS // SSM_GROUPS
    first = pl.program_id(1) == 0

    @pl.when(first)
    def _():
        state_scr[...] = jnp.zeros_like(state_scr)

    raw = xbc_ref[...].astype(F32)
    halo = jnp.where(first, 0.0, halo_ref[...].astype(F32))
    ext_scr[0:HALO, :] = halo
    ext_scr[HALO:HALO + l, :] = raw
    cw = cw_ref[...]
    conv = (cw[0:1] * ext_scr[HALO - 3:HALO - 3 + l, :]
            + cw[1:2] * ext_scr[HALO - 2:HALO - 2 + l, :]
            + cw[2:3] * ext_scr[HALO - 1:HALO - 1 + l, :]
            + cw[3:4] * raw)
    xbc = _silu(conv + cb_ref[...])
    xs = xbc[:, 0:GROUP_W]
    bm = xbc[:, GROUP_W:GROUP_W + SSM_GROUPS * n]
    cm = xbc[:, GROUP_W + SSM_GROUPS * n:GROUP_W + 2 * SSM_GROUPS * n]

    dt = _softplus(dt_ref[...].astype(F32) + dtb_ref[...])
    da = dt * (-jnp.exp(alog_ref[...]))
    ri = lax.broadcasted_iota(jnp.int32, (l, l), 0)
    ci = lax.broadcasted_iota(jnp.int32, (l, l), 1)
    causal = ri >= ci
    tril = jnp.where(causal, 1.0, 0.0).astype(BF16)
    da_hi, da_lo = _split_bf16(da)
    cum = _dot(tril, da_hi) + _dot(tril, da_lo)
    cum_t = cum.T
    last = cum[l - 1:l, :]
    dsk = dsk_ref[...]

    for gi in range(SSM_GROUPS):
        bg = bm[:, gi * n:(gi + 1) * n]
        cg = cm[:, gi * n:(gi + 1) * n].astype(BF16)
        bg_t = bg.T.astype(BF16)
        cb = _dot(cg, bg_t)
        for hi_ in range(rep):
            h = gi * rep + hi_
            col = cum[:, h:h + 1]
            row = cum_t[h:h + 1, :]
            decay = jnp.exp(jnp.where(causal, col - row, -jnp.inf))
            xh = xs[:, h * hd:(h + 1) * hd]
            xdt = xh * dt[:, h:h + 1]
            y = _dot((cb * decay).astype(BF16), xdt.astype(BF16))
            prev = state_scr[h]
            y = y + _dot(cg, prev.astype(BF16)) * jnp.exp(col)
            y = y + xh * dsk[0:1, h:h + 1]
            y_scr[:, h * hd:(h + 1) * hd] = y
            lh = last[0:1, h:h + 1]
            to_end = jnp.exp(lh - col)
            new = _dot(bg_t, (xdt * to_end).astype(BF16))
            state_scr[h] = prev * jnp.exp(lh) + new

    y = y_scr[...] * _silu(z_ref[...].astype(F32))
    o_ref[...] = _rms(y, g_ref[...]).astype(o_ref.dtype)


def _mixer_c(proj, conv_w, conv_b, dt_bias, a_log, d_skip, norm_g, batch, seq):
    t = proj.shape[0]
    l = SSM_CHUNK
    nc = seq // l
    wx = GROUP_W + 2 * SSM_GROUPS * SSM_STATE
    hb = l // HALO
    pad = lambda a: jnp.zeros((1, LANE), F32).at[0, :N_HEADS].set(a)
    cur = lambda u: (lambda b, c: (b * nc + c, u))
    return pl.pallas_call(
        _ssd_kernel,
        grid=(batch, nc),
        in_specs=[
            pl.BlockSpec((l, GROUP_W), cur(U_Z * LANE // GROUP_W)),
            pl.BlockSpec((l, wx), cur(U_XBC * LANE // wx)),
            pl.BlockSpec((HALO, wx), lambda b, c: (jnp.maximum((b * nc + c) * hb - 1, 0), U_XBC * LANE // wx)),
            pl.BlockSpec((l, LANE), cur(U_DT)),
            pl.BlockSpec(conv_w.shape, lambda b, c: (0, 0)),
            pl.BlockSpec((1, wx), lambda b, c: (0, 0)),
            pl.BlockSpec((1, LANE), lambda b, c: (0, 0)),
            pl.BlockSpec((1, LANE), lambda b, c: (0, 0)),
            pl.BlockSpec((1, LANE), lambda b, c: (0, 0)),
            pl.BlockSpec((1, GROUP_W), lambda b, c: (0, 0)),
        ],
        out_specs=pl.BlockSpec((l, GROUP_W), lambda b, c: (b * nc + c, 0)),
        out_shape=jax.ShapeDtypeStruct((t, GROUP_W), BF16),
        scratch_shapes=[
            pltpu.VMEM((HALO + l, wx), F32),
            pltpu.VMEM((l, GROUP_W), F32),
            pltpu.VMEM((N_HEADS, SSM_STATE, HEAD_DIM), F32),
        ],
        compiler_params=_cparams(("arbitrary", "arbitrary")),
        name="mixer_c_ssd",
    )(proj, proj, proj, proj, conv_w, conv_b.reshape(1, wx), pad(dt_bias), pad(a_log), pad(d_skip),
      norm_g.reshape(1, GROUP_W))


def _sb_kernel(q_ref, k_ref, v_ref, o_ref):
    kb_rows = SB_KEYS
    tq = q_ref.shape[0]
    nr = tq // kb_rows
    hd = HEAD_DIM
    nh = LANE // hd
    qi = pl.program_id(2)
    ri = lax.broadcasted_iota(jnp.int32, (kb_rows, kb_rows), 0)
    ci = lax.broadcasted_iota(jnp.int32, (kb_rows, kb_rows), 1)
    strict = ci < ri
    later = jnp.where(ri > ci, 1.0, 0.0).astype(BF16)
    lanes = [slice(h * hd, (h + 1) * hd) for h in range(nh)]
    qscale = math.log2(math.e) / math.sqrt(hd)
    qs = [[(q_ref[r * kb_rows:(r + 1) * kb_rows, ln].astype(F32) * qscale).astype(BF16) for r in range(nr)]
          for ln in lanes]
    sign = jnp.uint32(0x80000000)

    def stages(chains):
        zs = [lax.dot_general(q, k, (((1,), (1,)), ((), ())), preferred_element_type=F32)
              for q, k, _, _, _, _ in chains]
        sps = []
        for (_, _, _, keep, _, _), z in zip(chains, zs):
            neg_abs = lax.bitcast_convert_type(lax.bitcast_convert_type(z, jnp.uint32) | sign, F32)
            sp = jnp.maximum(z, 0.0) + jnp.log(1.0 + jnp.exp2(neg_abs)) * math.log2(math.e)
            sps.append(sp if keep is None else jnp.where(keep, sp, 0.0))
        afters = [_dot(sp.astype(BF16), later) for sp in sps]
        probs = []
        for (_, _, _, keep, _, _), z, sp, after in zip(chains, zs, sps, afters):
            a = jnp.exp2(z - sp - after)
            probs.append((a if keep is None else jnp.where(keep, a, 0.0)).astype(BF16))
        outs = [_dot(a, v) for (_, _, v, _, _, _), a in zip(chains, probs)]
        return [(run + jnp.sum(sp, axis=-1, keepdims=True), acc + jnp.exp2(-run) * o)
                for (_, _, _, _, run, acc), sp, o in zip(chains, sps, outs)]

    def step(kb, carry, kinds):
        rows = pl.ds(pl.multiple_of(kb * kb_rows, kb_rows), kb_rows)
        chains, where = [], []
        for h in range(nh):
            k = k_ref[rows, lanes[h]].astype(BF16)
            v = v_ref[rows, lanes[h]].astype(BF16)
            for r in range(nr):
                if kinds[r] is not None:
                    run, acc = carry[h * nr + r]
                    chains.append((qs[h][r], k, v, strict if kinds[r] == "diag" else None, run, acc))
                    where.append(h * nr + r)
        carry = list(carry)
        for i, rc in zip(where, stages(chains)):
            carry[i] = rc
        return tuple(carry)

    carry = tuple((jnp.zeros((kb_rows, 1), F32), jnp.zeros((kb_rows, hd), F32)) for _ in range(nh * nr))
    for d in reversed(range(nr)):
        kinds = [None if r < d else ("diag" if r == d else "full") for r in range(nr)]
        carry = step(qi * nr + d, carry, kinds)
    nfull = qi * nr
    carry = lax.fori_loop(0, nfull, lambda j, c: step(nfull - 1 - j, c, ["full"] * nr), carry)
    for h in range(nh):
        for r in range(nr):
            o_ref[r * kb_rows:(r + 1) * kb_rows, lanes[h]] = carry[h * nr + r][1].astype(o_ref.dtype)


def _mixer_d(proj, batch, seq):
    t = proj.shape[0]
    tq = min(2 * SB_KEYS, seq)
    nq = seq // tq
    pairs = GROUP_W // LANE
    return pl.pallas_call(
        _sb_kernel,
        grid=(batch, pairs, nq),
        in_specs=[
            pl.BlockSpec((tq, LANE), lambda b, p, i: (b * nq + i, U_DQ + p)),
            pl.BlockSpec((seq, LANE), lambda b, p, i: (b, U_DK + p)),
            pl.BlockSpec((seq, LANE), lambda b, p, i: (b, U_DV + p)),
        ],
        out_specs=pl.BlockSpec((tq, LANE), lambda b, p, i: (b * nq + i, p)),
        out_shape=jax.ShapeDtypeStruct((t, GROUP_W), BF16),
        compiler_params=_cparams(("arbitrary", "arbitrary", "arbitrary")),
        name="mixer_d_stickbreaking",
    )(proj, proj, proj)


def _outproj_kernel(ya_ref, yb_ref, yc_ref, yd_ref, gnd_ref, w_ref, x_ref, gate_ref, sh_ref, sc_ref, g_ref,
                    rw_ref, xo_ref, h_ref, lg_ref):
    w = GROUP_W
    yd = _rms(yd_ref[...].astype(F32), gnd_ref[...]).astype(BF16)
    mix = (_dot(ya_ref[...], w_ref[0:w, :]) + _dot(yb_ref[...], w_ref[w:2 * w, :])
           + _dot(yc_ref[...], w_ref[2 * w:3 * w, :]) + _dot(yd, w_ref[3 * w:4 * w, :]))
    x = x_ref[...] + gate_ref[0] * mix
    xo_ref[...] = x
    h = _rms(x, g_ref[...]) * (1.0 + sc_ref[0]) + sh_ref[0]
    for c in range(ROW_CHUNKS):
        h_ref[pl.ds(c, h.shape[0], stride=ROW_CHUNKS), :] = h[:, c * LANE:(c + 1) * LANE]
    lg_ref[...] = _dot3(h, rw_ref[...]).T[0:N_EXPERTS, :]


def _out_projection(ya, yb, yc, yd, gn_d, w_out_b, layer, x2, mod6, g2, router_w_p, seq):
    t, d = x2.shape
    tm = min(256, seq)
    per_b = seq // tm
    row = lambda i: (i, 0)
    const = lambda i: (0, 0)
    modspec = lambda m: pl.BlockSpec((1, 1, d), lambda i: ((i // per_b) * N_MOD + m, 0, 0))
    return pl.pallas_call(
        _outproj_kernel,
        grid=(t // tm,),
        in_specs=[
            pl.BlockSpec((tm, GROUP_W), row), pl.BlockSpec((tm, GROUP_W), row),
            pl.BlockSpec((tm, GROUP_W), row), pl.BlockSpec((tm, GROUP_W), row),
            pl.BlockSpec((1, GROUP_W), const),
            pl.BlockSpec((None, d, d), lambda i: (layer, 0, 0)),
            pl.BlockSpec((tm, d), row),
            modspec(2), modspec(3), modspec(4),
            pl.BlockSpec((1, d), const),
            pl.BlockSpec((d, LANE), const),
        ],
        out_specs=[
            pl.BlockSpec((tm, d), row),
            pl.BlockSpec((tm * ROW_CHUNKS, LANE), row),
            pl.BlockSpec((N_EXPERTS, tm), lambda i: (0, i)),
        ],
        out_shape=[
            jax.ShapeDtypeStruct((t, d), F32),
            jax.ShapeDtypeStruct((t * ROW_CHUNKS, LANE), F32),
            jax.ShapeDtypeStruct((N_EXPERTS, t), F32),
        ],
        compiler_params=_cparams(("arbitrary",)),
        name="outproj_adaln_router",
    )(ya, yb, yc, yd, gn_d.reshape(1, GROUP_W), w_out_b, x2, mod6, mod6, mod6, g2.reshape(1, d), router_w_p)


def _route_kernel(lg_ref, bias_ref, oi_ref, of_ref, cnt_ref):
    ne = N_EXPERTS
    rt = lg_ref.shape[1]

    @pl.when(pl.program_id(0) == 0)
    def _():
        cnt_ref[...] = jnp.zeros_like(cnt_ref)

    lg = lg_ref[...]
    e = jnp.exp(lg - jnp.max(lg, axis=0, keepdims=True))
    probs = e / jnp.sum(e, axis=0, keepdims=True)
    sel = probs + bias_ref[...][:, 0:1]
    rowi = lax.broadcasted_iota(jnp.int32, (ne, rt), 0)
    neg = -jnp.inf

    gscore = []
    for g in range(N_EXPERT_GROUPS):
        r = [sel[g * EXPERTS_PER_GROUP + i:g * EXPERTS_PER_GROUP + i + 1, :] for i in range(EXPERTS_PER_GROUP)]
        best = None
        for i in range(EXPERTS_PER_GROUP):
            for j in range(i + 1, EXPERTS_PER_GROUP):
                pair = jnp.maximum(r[i], r[j]) + jnp.minimum(r[i], r[j])
                best = pair if best is None else jnp.maximum(best, pair)
        gscore.append(best)
    top_g = jnp.zeros((1, rt), jnp.int32)
    top_s = gscore[0]
    for g in range(1, N_EXPERT_GROUPS):
        better = gscore[g] > top_s
        top_g = jnp.where(better, g, top_g)
        top_s = jnp.where(better, gscore[g], top_s)

    in_grp = (rowi // EXPERTS_PER_GROUP) == top_g
    masked = jnp.where(in_grp, sel, neg)
    m1 = jnp.max(masked, axis=0, keepdims=True)
    i1 = jnp.min(jnp.where(masked == m1, rowi, ne), axis=0, keepdims=True)
    masked2 = jnp.where(rowi == i1, neg, masked)
    m2 = jnp.max(masked2, axis=0, keepdims=True)
    i2 = jnp.min(jnp.where(masked2 == m2, rowi, ne), axis=0, keepdims=True)
    hit1 = rowi == i1
    hit2 = rowi == i2
    g1 = jnp.sum(jnp.where(hit1, probs, 0.0), axis=0, keepdims=True)
    g2 = jnp.sum(jnp.where(hit2, probs, 0.0), axis=0, keepdims=True)
    gsum = g1 + g2

    onehot = jnp.where(hit1 | hit2, 1.0, 0.0)
    ji = lax.broadcasted_iota(jnp.int32, (rt, rt), 0)
    ti = lax.broadcasted_iota(jnp.int32, (rt, rt), 1)
    before = jnp.where(ji < ti, 1.0, 0.0).astype(BF16)
    base = cnt_ref[...][:, 0:1]
    prior = _dot(onehot.astype(BF16), before) + base
    r1 = jnp.sum(jnp.where(hit1, prior, 0.0), axis=0, keepdims=True)
    r2 = jnp.sum(jnp.where(hit2, prior, 0.0), axis=0, keepdims=True)
    cnt_ref[...] = cnt_ref[...] + jnp.sum(onehot, axis=1, keepdims=True)

    zi = jnp.zeros((4, rt), jnp.int32)
    oi_ref[...] = jnp.concatenate([i1, i2, r1.astype(jnp.int32), r2.astype(jnp.int32), zi], axis=0)
    zf = jnp.zeros((6, rt), F32)
    of_ref[...] = jnp.concatenate([g1 / gsum, g2 / gsum, zf], axis=0)


def _routing(logits_t, router_bias):
    ne, t = logits_t.shape
    rt = min(512, t)
    bias = jnp.broadcast_to(router_bias.astype(F32)[:, None], (ne, LANE))
    return pl.pallas_call(
        _route_kernel,
        grid=(t // rt,),
        in_specs=[pl.BlockSpec((ne, rt), lambda i: (0, i)), pl.BlockSpec((ne, LANE), lambda i: (0, 0))],
        out_specs=[
            pl.BlockSpec((8, rt), lambda i: (0, i)),
            pl.BlockSpec((8, rt), lambda i: (0, i)),
            pl.BlockSpec((ne, LANE), lambda i: (0, 0)),
        ],
        out_shape=[
            jax.ShapeDtypeStruct((8, t), jnp.int32),
            jax.ShapeDtypeStruct((8, t), F32),
            jax.ShapeDtypeStruct((ne, LANE), F32),
        ],
        compiler_params=_cparams(("arbitrary",)),
        name="moe_routing",
    )(logits_t, bias)


def _moe_kernel(blk_e, src_tok, dst_row, h_hbm, wg_ref, wu_ref, wd_ref, out_hbm, xbuf, ybuf, gsem, ssem):
    i = pl.program_id(0)
    nblk = pl.num_programs(0)
    rc = ROW_CHUNKS
    rb = xbuf.shape[1] // rc
    slot = lax.rem(i, MOE_SLOTS)
    slot_next2 = lax.rem(i + 2, MOE_SLOTS)

    def gather_start(plan_blk, s, j):
        tok = pl.multiple_of(src_tok[plan_blk * rb + j] * rc, rc)
        pltpu.make_async_copy(h_hbm.at[pl.ds(tok, rc), :], xbuf.at[s, pl.ds(j * rc, rc), :], gsem.at[s]).start()

    def scatter_start(plan_blk, s, j):
        row = pl.multiple_of(dst_row[plan_blk * rb + j] * rc, rc)
        pltpu.make_async_copy(ybuf.at[s, pl.ds(j * rc, rc), :], out_hbm.at[pl.ds(row, rc), :], ssem.at[s]).start()

    def gather_wait(s):
        pltpu.make_async_copy(h_hbm.at[pl.ds(0, rb * rc), :], xbuf.at[s], gsem.at[s]).wait()

    def scatter_wait(s):
        pltpu.make_async_copy(ybuf.at[s], out_hbm.at[pl.ds(0, rb * rc), :], ssem.at[s]).wait()

    @pl.when(i == 0)
    def _():
        ybuf[...] = jnp.zeros_like(ybuf)
        for j in range(rb):
            gather_start(1, 0, j)
            gather_start(2, 1, j)

    gather_wait(slot)

    @pl.when(i >= 2)
    def _():
        scatter_wait(slot)

    x = jnp.concatenate([xbuf[slot, pl.ds(c, rb, stride=rc), :] for c in range(rc)], axis=1).astype(BF16)
    for j in range(rb):
        gather_start(i + 3, slot_next2, j)
        scatter_start(i, slot_next2, j)
    hid = _silu(_dot(x, wg_ref[0])) * _dot(x, wu_ref[0])
    y = _dot(hid.astype(BF16), wd_ref[0])
    for c in range(rc):
        ybuf[slot, pl.ds(c, rb, stride=rc), :] = y[:, c * LANE:(c + 1) * LANE]

    @pl.when(i == nblk - 1)
    def _():
        for j in range(rb):
            scatter_start(i + 1, slot, j)
        for s in range(MOE_SLOTS):
            scatter_wait(s)
        gather_wait(lax.rem(i + 1, MOE_SLOTS))
        gather_wait(slot_next2)


def _experts(blk_e, src_tok, dst_row, h2, wg, wu, wd, layer, rb):
    rc = ROW_CHUNKS
    _, ne, d, dff = wg.shape
    nblk = blk_e.shape[0]
    out_rows = (nblk + 1) * rb
    return pl.pallas_call(
        _moe_kernel,
        grid_spec=pltpu.PrefetchScalarGridSpec(
            num_scalar_prefetch=3,
            grid=(nblk,),
            in_specs=[
                pl.BlockSpec(memory_space=pl.ANY),
                pl.BlockSpec((None, 1, d, dff), lambda i, be, st, dr: (layer, be[i], 0, 0)),
                pl.BlockSpec((None, 1, d, dff), lambda i, be, st, dr: (layer, be[i], 0, 0)),
                pl.BlockSpec((None, 1, dff, d), lambda i, be, st, dr: (layer, be[i], 0, 0)),
            ],
            out_specs=pl.BlockSpec(memory_space=pl.ANY),
            scratch_shapes=[
                pltpu.VMEM((MOE_SLOTS, rb * rc, LANE), F32),
                pltpu.VMEM((MOE_SLOTS, rb * rc, LANE), F32),
                pltpu.SemaphoreType.DMA((MOE_SLOTS,)),
                pltpu.SemaphoreType.DMA((MOE_SLOTS,)),
            ],
        ),
        out_shape=jax.ShapeDtypeStruct((out_rows * rc, LANE), F32),
        compiler_params=_cparams(("arbitrary",)),
        name="moe_experts",
    )(blk_e, src_tok, dst_row, h2, wg, wu, wd)


def _combine_kernel(x_ref, y0_ref, y1_ref, w_ref, gate_ref, g_ref, o_ref, *, final):
    w = w_ref[...]
    tm = x_ref.shape[0]
    rows = lambda ref: jnp.concatenate([ref[pl.ds(c, tm, stride=ROW_CHUNKS), :] for c in range(ROW_CHUNKS)], axis=1)
    moe = w[:, 0:1] * rows(y0_ref) + w[:, 1:2] * rows(y1_ref)
    x = x_ref[...] + gate_ref[0] * moe
    if final:
        x = _rms(x, g_ref[...])
    o_ref[...] = x


def _combine(x2, y2, gates_t, mod6, final_g, seq, final):
    t, d = x2.shape
    tm = min(512, seq)
    per_b = seq // tm
    nt = t // tm
    return pl.pallas_call(
        functools.partial(_combine_kernel, final=final),
        grid=(nt,),
        in_specs=[
            pl.BlockSpec((tm, d), lambda i: (i, 0)),
            pl.BlockSpec((tm * ROW_CHUNKS, LANE), lambda i: (i, 0)),
            pl.BlockSpec((tm * ROW_CHUNKS, LANE), lambda i: (nt + i, 0)),
            pl.BlockSpec((tm, 2), lambda i: (i, 0)),
            pl.BlockSpec((1, 1, d), lambda i: ((i // per_b) * N_MOD + 5, 0, 0)),
            pl.BlockSpec((1, d), lambda i: (0, 0)),
        ],
        out_specs=pl.BlockSpec((tm, d), lambda i: (i, 0)),
        out_shape=jax.ShapeDtypeStruct((t, d), F32),
        compiler_params=_cparams(("arbitrary",)),
        name="moe_combine",
    )(x2, y2, y2, gates_t, mod6, final_g.reshape(1, d))


def _permute_w_in(w_in):
    depth, d, _ = w_in.shape
    gw = GROUP_W
    kv = SWA_KV_HEADS * HEAD_DIM
    xbc = gw + 2 * SSM_GROUPS * SSM_STATE
    o = 0
    a = w_in[:, :, o:o + 3 * gw]; o += 3 * gw
    bq = w_in[:, :, o:o + gw]; o += gw
    bk = w_in[:, :, o:o + kv]; o += kv
    bv = w_in[:, :, o:o + kv]; o += kv
    cz = w_in[:, :, o:o + gw]; o += gw
    cx = w_in[:, :, o:o + xbc]; o += xbc
    cdt = w_in[:, :, o:o + N_HEADS]; o += N_HEADS
    dqkv = w_in[:, :, o:o + 3 * gw]; o += 3 * gw
    zeros = lambda n: jnp.zeros((depth, d, n), w_in.dtype)
    out = jnp.concatenate([a, dqkv, cz, bq, cx, bk, bv, cdt, zeros(LANE - N_HEADS), zeros(LANE)], axis=-1)
    return out.astype(BF16)


def _dispatch_plan(idx, cnt, t, rb):
    ne = N_EXPERTS
    counts = cnt[:, 0].astype(jnp.int32)
    padded = (counts + rb - 1) // rb * rb
    pad_ends = jnp.cumsum(padded)
    pad_starts = pad_ends - padded
    e_idx = idx[0:2]
    rank = idx[2:4]
    onehot = e_idx[:, :, None] == jnp.arange(ne, dtype=jnp.int32)
    dest = jnp.sum(jnp.where(onehot, pad_starts, 0), axis=-1) + rank + rb
    nblk = (2 * t + ne * rb) // rb
    npos = (nblk + 3) * rb
    tok = jnp.broadcast_to(jnp.arange(t, dtype=jnp.int32)[None, :], (2, t))
    out_row = jnp.arange(2, dtype=jnp.int32)[:, None] * t + tok
    placed = jnp.full((npos,), -1, jnp.int32).at[dest.reshape(-1)].set(out_row.reshape(-1))
    is_pad = placed < 0
    dump_row = 2 * t + jnp.cumsum(is_pad.astype(jnp.int32)) - 1
    dst_row = jnp.where(is_pad, dump_row, placed)
    src_tok = jnp.where(is_pad, 0, jnp.where(placed >= t, placed - t, placed))
    blk_start = jnp.arange(nblk, dtype=jnp.int32) * rb
    blk_e = jnp.minimum(jnp.sum(pad_ends[None, :] <= blk_start[:, None], axis=1), ne - 1).astype(jnp.int32)
    return blk_e, src_tok, dst_row


def kernel(x, c, w_mod, b_mod, norm1_g, norm2_g, w_in, w_out, conv_a_w, gn_a, attn_sinks, gn_b, ssm_conv_w,
           ssm_conv_b, dt_bias, a_log, d_skip, ssm_norm_g, gn_d, router_w, router_bias, moe_w_gate, moe_w_up,
           moe_w_down, final_g):
    batch, seq, d = x.shape
    assert d == ROW_CHUNKS * LANE
    depth = w_in.shape[0]
    t = batch * seq
    rb = 256

    mod = _modulation(c, w_mod, b_mod)
    w_in_p = _permute_w_in(w_in)
    w_out_b = w_out.astype(BF16)
    wg_b, wu_b, wd_b = moe_w_gate.astype(BF16), moe_w_up.astype(BF16), moe_w_down.astype(BF16)
    router_w_p = jnp.zeros((d, LANE), F32).at[:, :N_EXPERTS].set(router_w)

    x2 = x.reshape(t, d)
    for l in range(depth):
        mod6 = mod[l].reshape(batch * N_MOD, 1, d)
        proj = _in_projection(x2, norm1_g[l], mod6, w_in_p, l, seq)
        ya = _mixer_a(proj, conv_a_w[l], gn_a[l], batch, seq)
        yb = _mixer_b(proj, attn_sinks[l], gn_b[l], batch, seq)
        yc = _mixer_c(proj, ssm_conv_w[l], ssm_conv_b[l], dt_bias[l], a_log[l], d_skip[l], ssm_norm_g[l],
                      batch, seq)
        yd = _mixer_d(proj, batch, seq)
        x2, h2, logits_t = _out_projection(ya, yb, yc, yd, gn_d[l], w_out_b, l, x2, mod6, norm2_g[l],
                                           router_w_p, seq)
        idx, gates, cnt = _routing(logits_t, router_bias)
        blk_e, src_tok, dst_row = _dispatch_plan(idx, cnt, t, rb)
        y2 = _experts(blk_e, src_tok, dst_row, h2, wg_b, wu_b, wd_b, l, rb)
        x2 = _combine(x2, y2, gates[0:2].T, mod6, final_g, seq, final=(l == depth - 1))
    return x2.reshape(batch, seq, d)
```

```python
import functools
import math

import jax
import jax.numpy as jnp
from jax import lax
from jax.experimental import pallas as pl
from jax.experimental.pallas import tpu as pltpu

F32 = jnp.float32
BF16 = jnp.bfloat16

EPS = 1e-6
HEAD_DIM = 64
GROUP_W = 512
N_HEADS = GROUP_W // HEAD_DIM
SWA_KV_HEADS = 2
SWA_WINDOW = 128
SSM_GROUPS = 2
SSM_STATE = 128
SSM_CHUNK = 128
N_EXPERTS = 16
N_EXPERT_GROUPS = 4
EXPERTS_PER_GROUP = N_EXPERTS // N_EXPERT_GROUPS
N_MOD = 6

LANE = 128
HALO = 16
SB_KEYS = 256
ROW_CHUNKS = 16
MOE_SLOTS = 3
VMEM_LIMIT = 56 * 1024 * 1024

U_A = 0
U_DQ = 12
U_DK = 16
U_DV = 20
U_Z = 24
U_BQ = 28
U_XBC = 32
U_BK = 40
U_BV = 41
U_DT = 42
N_UNITS = 44
NP = N_UNITS * LANE

PROJ_DTYPE = jnp.bfloat16


def _cparams(sem, vmem=VMEM_LIMIT):
    return pltpu.CompilerParams(dimension_semantics=sem, vmem_limit_bytes=vmem)


def _rms(y, g):
    return y * lax.rsqrt(jnp.mean(y * y, axis=-1, keepdims=True) + EPS) * g


def _split_bf16(a):
    hi = a.astype(BF16)
    lo = (a - hi.astype(F32)).astype(BF16)
    return hi, lo


def _dot(a, b):
    return jnp.dot(a, b, preferred_element_type=F32)


def _dot3(a, b):
    ah, al = _split_bf16(a)
    bh, bl = _split_bf16(b)
    return _dot(ah, bh) + _dot(al, bh) + _dot(ah, bl)


def _silu(a):
    return a * (1.0 / (1.0 + jnp.exp(-a)))


def _softplus(a):
    return jnp.maximum(a, 0.0) + jnp.log1p(jnp.exp(-jnp.abs(a)))


def _mod_kernel(c_ref, w_ref, b_ref, o_ref):
    cond = _silu(c_ref[...])
    o_ref[0] = _dot3(cond, w_ref[0]) + b_ref[0]


def _modulation(c, w_mod, b_mod):
    depth, d, nm = w_mod.shape
    b = c.shape[0]
    rows = 8
    cp = jnp.zeros((rows, d), F32).at[:b].set(c)
    tn = 1024
    out = pl.pallas_call(
        _mod_kernel,
        grid=(depth, nm // tn),
        in_specs=[
            pl.BlockSpec((rows, d), lambda l, j: (0, 0)),
            pl.BlockSpec((1, d, tn), lambda l, j: (l, 0, j)),
            pl.BlockSpec((1, 1, tn), lambda l, j: (l, 0, j)),
        ],
        out_specs=pl.BlockSpec((1, rows, tn), lambda l, j: (l, 0, j)),
        out_shape=jax.ShapeDtypeStruct((depth, rows, nm), F32),
        compiler_params=_cparams(("arbitrary", "arbitrary")),
        name="modulation",
    )(cp, w_mod, b_mod.reshape(depth, 1, nm))
    return out[:, :b]


def _inproj_kernel(x_ref, g_ref, sh_ref, sc_ref, w_ref, o_ref, h_scr):
    @pl.when(pl.program_id(1) == 0)
    def _():
        chunk = LANE

        def body(r, carry):
            rows = pl.ds(pl.multiple_of(r * chunk, chunk), chunk)
            h = _rms(x_ref[rows, :], g_ref[...]) * (1.0 + sc_ref[0]) + sh_ref[0]
            h_scr[rows, :] = h.astype(BF16)
            return carry

        lax.fori_loop(0, x_ref.shape[0] // chunk, body, 0)

    o_ref[...] = _dot(h_scr[...], w_ref[...]).astype(o_ref.dtype)


def _in_projection(x2, g, mod6, w_in_p, layer, seq):
    t, d = x2.shape
    tm = min(1024, seq)
    tn = 512
    per_b = seq // tm
    return pl.pallas_call(
        _inproj_kernel,
        grid=(t // tm, NP // tn),
        in_specs=[
            pl.BlockSpec((tm, d), lambda i, j: (i, 0)),
            pl.BlockSpec((1, d), lambda i, j: (0, 0)),
            pl.BlockSpec((1, 1, d), lambda i, j: ((i // per_b) * N_MOD + 0, 0, 0)),
            pl.BlockSpec((1, 1, d), lambda i, j: ((i // per_b) * N_MOD + 1, 0, 0)),
            pl.BlockSpec((None, d, tn), lambda i, j: (layer, 0, j)),
        ],
        out_specs=pl.BlockSpec((tm, tn), lambda i, j: (i, j)),
        out_shape=jax.ShapeDtypeStruct((t, NP), PROJ_DTYPE),
        scratch_shapes=[pltpu.VMEM((tm, d), BF16)],
        compiler_params=_cparams(("arbitrary", "arbitrary")),
        name="adaln_inproj",
    )(x2, g.reshape(1, d), mod6, mod6, w_in_p)


def _mixa_kernel(p_ref, halo_ref, w_ref, g_ref, o_ref, ext_scr):
    ts = p_ref.shape[0]
    w = GROUP_W
    p = p_ref[...].astype(F32)
    hp = halo_ref[...].astype(F32)
    u = p[:, w:2 * w] * p[:, 2 * w:3 * w]
    hu = hp[:, w:2 * w] * hp[:, 2 * w:3 * w]
    hu = jnp.where(pl.program_id(1) == 0, 0.0, hu)
    ext_scr[0:HALO, :] = hu
    ext_scr[HALO:HALO + ts, :] = u
    cw = w_ref[...]
    conv = (cw[0:1] * ext_scr[HALO - 2:HALO - 2 + ts, :]
            + cw[1:2] * ext_scr[HALO - 1:HALO - 1 + ts, :]
            + cw[2:3] * u)
    o_ref[...] = _rms(p[:, 0:w] * conv, g_ref[...]).astype(o_ref.dtype)


def _mixer_a(proj, conv_w, gn, batch, seq):
    t = proj.shape[0]
    ts = min(512, seq)
    nt = seq // ts
    hb = ts // HALO
    wa = 3 * GROUP_W
    return pl.pallas_call(
        _mixa_kernel,
        grid=(batch, nt),
        in_specs=[
            pl.BlockSpec((ts, wa), lambda b, i: (b * nt + i, U_A * LANE // wa)),
            pl.BlockSpec((HALO, wa), lambda b, i: (jnp.maximum((b * nt + i) * hb - 1, 0), U_A * LANE // wa)),
            pl.BlockSpec(conv_w.shape, lambda b, i: (0, 0)),
            pl.BlockSpec((1, GROUP_W), lambda b, i: (0, 0)),
        ],
        out_specs=pl.BlockSpec((ts, GROUP_W), lambda b, i: (b * nt + i, 0)),
        out_shape=jax.ShapeDtypeStruct((t, GROUP_W), BF16),
        scratch_shapes=[pltpu.VMEM((HALO + ts, GROUP_W), F32)],
        compiler_params=_cparams(("arbitrary", "arbitrary")),
        name="mixer_a_conv",
    )(proj, proj, conv_w, gn.reshape(1, GROUP_W))


def _swa_kernel(q_ref, k_ref, kp_ref, v_ref, vp_ref, sink_ref, slope_ref, g_ref, o_ref, y_scr):
    w = SWA_WINDOW
    hd = HEAD_DIM
    grp = N_HEADS // SWA_KV_HEADS
    nblk = q_ref.shape[0] // w
    first = pl.program_id(1) == 0
    q = (q_ref[...].astype(F32) * (1.0 / math.sqrt(hd))).astype(BF16)
    k = jnp.concatenate([kp_ref[...], k_ref[...]], axis=0).astype(BF16)
    v = jnp.concatenate([vp_ref[...], v_ref[...]], axis=0).astype(BF16)
    rows = grp * w
    qi = lax.broadcasted_iota(jnp.int32, (rows, 2 * w), 0) % w
    kj = lax.broadcasted_iota(jnp.int32, (rows, 2 * w), 1)
    hrow = lax.broadcasted_iota(jnp.int32, (rows, 1), 0) // w
    dist = qi + w - kj
    valid = (dist >= 0) & (dist < w)
    distf = dist.astype(F32)
    sinks = sink_ref[...]
    slopes = slope_ref[...]

    chains = []
    for kvh in range(SWA_KV_HEADS):
        slope = jnp.zeros((rows, 1), F32)
        sink = jnp.zeros((rows, 1), F32)
        for gi in range(grp):
            h = kvh * grp + gi
            slope = jnp.where(hrow == gi, slopes[0:1, h:h + 1], slope)
            sink = jnp.where(hrow == gi, sinks[0:1, h:h + 1], sink)
        bias = jnp.where(valid, -slope * distf, -jnp.inf)
        bias0 = jnp.where(first & (kj < w), -jnp.inf, bias)
        for n in range(nblk):
            qs = jnp.concatenate([q[n * w:(n + 1) * w, (kvh * grp + gi) * hd:(kvh * grp + gi + 1) * hd]
                                  for gi in range(grp)], axis=0)
            kn = k[n * w:(n + 2) * w, kvh * hd:(kvh + 1) * hd]
            vn = v[n * w:(n + 2) * w, kvh * hd:(kvh + 1) * hd]
            chains.append((kvh, n, qs, kn, vn, bias0 if n == 0 else bias, sink))

    scores = [lax.dot_general(qs, kn, (((1,), (1,)), ((), ())), preferred_element_type=F32) + bias
              for _, _, qs, kn, _, bias, _ in chains]
    es, rden = [], []
    for (_, _, _, _, _, _, sink), s in zip(chains, scores):
        m = jnp.maximum(jnp.max(s, axis=-1, keepdims=True), sink)
        e = jnp.exp(s - m)
        rden.append(1.0 / (jnp.sum(e, axis=-1, keepdims=True) + jnp.exp(sink - m)))
        es.append(e.astype(BF16))
    outs = [_dot(e, vn) * r for (_, _, _, _, vn, _, _), e, r in zip(chains, es, rden)]
    for (kvh, n, _, _, _, _, _), o in zip(chains, outs):
        for gi in range(grp):
            h = kvh * grp + gi
            y_scr[n * w:(n + 1) * w, h * hd:(h + 1) * hd] = o[gi * w:(gi + 1) * w, :]
    o_ref[...] = _rms(y_scr[...], g_ref[...]).astype(o_ref.dtype)


def _mixer_b(proj, sinks, gn, batch, seq):
    t = proj.shape[0]
    w = SWA_WINDOW
    ts = min(512, seq)
    nt = seq // ts
    pb = ts // w
    slopes = 2.0 ** (-8.0 * jnp.arange(1, N_HEADS + 1, dtype=F32) / N_HEADS)
    pad = lambda a: jnp.zeros((1, LANE), F32).at[0, :N_HEADS].set(a)
    cur = lambda u: (lambda b, i: (b * nt + i, u))
    prev = lambda u: (lambda b, i: (jnp.maximum((b * nt + i) * pb - 1, 0), u))
    return pl.pallas_call(
        _swa_kernel,
        grid=(batch, nt),
        in_specs=[
            pl.BlockSpec((ts, GROUP_W), cur(U_BQ * LANE // GROUP_W)),
            pl.BlockSpec((ts, LANE), cur(U_BK)),
            pl.BlockSpec((w, LANE), prev(U_BK)),
            pl.BlockSpec((ts, LANE), cur(U_BV)),
            pl.BlockSpec((w, LANE), prev(U_BV)),
            pl.BlockSpec((1, LANE), lambda b, i: (0, 0)),
            pl.BlockSpec((1, LANE), lambda b, i: (0, 0)),
            pl.BlockSpec((1, GROUP_W), lambda b, i: (0, 0)),
        ],
        out_specs=pl.BlockSpec((ts, GROUP_W), lambda b, i: (b * nt + i, 0)),
        out_shape=jax.ShapeDtypeStruct((t, GROUP_W), BF16),
        scratch_shapes=[pltpu.VMEM((ts, GROUP_W), F32)],
        compiler_params=_cparams(("arbitrary", "arbitrary")),
        name="mixer_b_swa",
    )(proj, proj, proj, proj, proj, pad(sinks), pad(slopes), gn.reshape(1, GROUP_W))


def _ssd_kernel(z_ref, xbc_ref, halo_ref, dt_ref, cw_ref, cb_ref, dtb_ref, alog_ref, dsk_ref, g_ref,
                o_ref, ext_scr, y_scr, state_scr):
    l = SSM_CHUNK
    hd = HEAD_DIM
    n = SSM_STATE
    rep = N_HEADS // SSM_GROUPS
    first = pl.program_id(1) == 0

    @pl.when(first)
    def _():
        state_scr[...] = jnp.zeros_like(state_scr)

    raw = xbc_ref[...].astype(F32)
    halo = jnp.where(first, 0.0, halo_ref[...].astype(F32))
    ext_scr[0:HALO, :] = halo
    ext_scr[HALO:HALO + l, :] = raw
    cw = cw_ref[...]
    conv = (cw[0:1] * ext_scr[HALO - 3:HALO - 3 + l, :]
            + cw[1:2] * ext_scr[HALO - 2:HALO - 2 + l, :]
            + cw[2:3] * ext_scr[HALO - 1:HALO - 1 + l, :]
            + cw[3:4] * raw)
    xbc = _silu(conv + cb_ref[...])
    xs = xbc[:, 0:GROUP_W]
    bm = xbc[:, GROUP_W:GROUP_W + SSM_GROUPS * n]
    cm = xbc[:, GROUP_W + SSM_GROUPS * n:GROUP_W + 2 * SSM_GROUPS * n]

    dt = _softplus(dt_ref[...].astype(F32) + dtb_ref[...])
    da = dt * (-jnp.exp(alog_ref[...]))
    ri = lax.broadcasted_iota(jnp.int32, (l, l), 0)
    ci = lax.broadcasted_iota(jnp.int32, (l, l), 1)
    causal = ri >= ci
    tril = jnp.where(causal, 1.0, 0.0).astype(BF16)
    da_hi, da_lo = _split_bf16(da)
    cum = _dot(tril, da_hi) + _dot(tril, da_lo)
    cum_t = cum.T
    last = cum[l - 1:l, :]
    dsk = dsk_ref[...]

    for gi in range(SSM_GROUPS):
        bg = bm[:, gi * n:(gi + 1) * n]
        cg = cm[:, gi * n:(gi + 1) * n].astype(BF16)
        bg_t = bg.T.astype(BF16)
        cb = _dot(cg, bg_t)
        for hi_ in range(rep):
            h = gi * rep + hi_
            col = cum[:, h:h + 1]
            row = cum_t[h:h + 1, :]
            decay = jnp.exp(jnp.where(causal, col - row, -jnp.inf))
            xh = xs[:, h * hd:(h + 1) * hd]
            xdt = xh * dt[:, h:h + 1]
            y = _dot((cb * decay).astype(BF16), xdt.astype(BF16))
            prev = state_scr[h]
            y = y + _dot(cg, prev.astype(BF16)) * jnp.exp(col)
            y = y + xh * dsk[0:1, h:h + 1]
            y_scr[:, h * hd:(h + 1) * hd] = y
            lh = last[0:1, h:h + 1]
            to_end = jnp.exp(lh - col)
            new = _dot(bg_t, (xdt * to_end).astype(BF16))
            state_scr[h] = prev * jnp.exp(lh) + new

    y = y_scr[...] * _silu(z_ref[...].astype(F32))
    o_ref[...] = _rms(y, g_ref[...]).astype(o_ref.dtype)


def _mixer_c(proj, conv_w, conv_b, dt_bias, a_log, d_skip, norm_g, batch, seq):
    t = proj.shape[0]
    l = SSM_CHUNK
    nc = seq // l
    wx = GROUP_W + 2 * SSM_GROUPS * SSM_STATE
    hb = l // HALO
    pad = lambda a: jnp.zeros((1, LANE), F32).at[0, :N_HEADS].set(a)
    cur = lambda u: (lambda b, c: (b * nc + c, u))
    return pl.pallas_call(
        _ssd_kernel,
        grid=(batch, nc),
        in_specs=[
            pl.BlockSpec((l, GROUP_W), cur(U_Z * LANE // GROUP_W)),
            pl.BlockSpec((l, wx), cur(U_XBC * LANE // wx)),
            pl.BlockSpec((HALO, wx), lambda b, c: (jnp.maximum((b * nc + c) * hb - 1, 0), U_XBC * LANE // wx)),
            pl.BlockSpec((l, LANE), cur(U_DT)),
            pl.BlockSpec(conv_w.shape, lambda b, c: (0, 0)),
            pl.BlockSpec((1, wx), lambda b, c: (0, 0)),
            pl.BlockSpec((1, LANE), lambda b, c: (0, 0)),
            pl.BlockSpec((1, LANE), lambda b, c: (0, 0)),
            pl.BlockSpec((1, LANE), lambda b, c: (0, 0)),
            pl.BlockSpec((1, GROUP_W), lambda b, c: (0, 0)),
        ],
        out_specs=pl.BlockSpec((l, GROUP_W), lambda b, c: (b * nc + c, 0)),
        out_shape=jax.ShapeDtypeStruct((t, GROUP_W), BF16),
        scratch_shapes=[
            pltpu.VMEM((HALO + l, wx), F32),
            pltpu.VMEM((l, GROUP_W), F32),
            pltpu.VMEM((N_HEADS, SSM_STATE, HEAD_DIM), F32),
        ],
        compiler_params=_cparams(("arbitrary", "arbitrary")),
        name="mixer_c_ssd",
    )(proj, proj, proj, proj, conv_w, conv_b.reshape(1, wx), pad(dt_bias), pad(a_log), pad(d_skip),
      norm_g.reshape(1, GROUP_W))


def _sb_kernel(q_ref, k_ref, v_ref, o_ref):
    kb_rows = SB_KEYS
    tq = q_ref.shape[0]
    nr = tq // kb_rows
    hd = HEAD_DIM
    nh = LANE // hd
    qi = pl.program_id(2)
    ri = lax.broadcasted_iota(jnp.int32, (kb_rows, kb_rows), 0)
    ci = lax.broadcasted_iota(jnp.int32, (kb_rows, kb_rows), 1)
    strict = ci < ri
    later = jnp.where(ri > ci, 1.0, 0.0).astype(BF16)
    lanes = [slice(h * hd, (h + 1) * hd) for h in range(nh)]
    qscale = math.log2(math.e) / math.sqrt(hd)
    qs = [[(q_ref[r * kb_rows:(r + 1) * kb_rows, ln].astype(F32) * qscale).astype(BF16) for r in range(nr)]
          for ln in lanes]
    sign = jnp.uint32(0x80000000)

    def stages(items):
        zs = [lax.dot_general(q, k, (((1,), (1,)), ((), ())), preferred_element_type=F32)
              for q, k, _, _ in items]
        sps = []
        for (_, _, _, keep), z in zip(items, zs):
            neg_abs = lax.bitcast_convert_type(lax.bitcast_convert_type(z, jnp.uint32) | sign, F32)
            sp = jnp.maximum(z, 0.0) + jnp.log(1.0 + jnp.exp2(neg_abs)) * math.log2(math.e)
            sps.append(sp if keep is None else jnp.where(keep, sp, 0.0))
        afters = [_dot(sp.astype(BF16), later) for sp in sps]
        probs = []
        for (_, _, _, keep), z, sp, after in zip(items, zs, sps, afters):
            a = jnp.exp2(z - sp - after)
            probs.append((a if keep is None else jnp.where(keep, a, 0.0)).astype(BF16))
        outs = [_dot(a, v) for (_, _, v, _), a in zip(items, probs)]
        return [(jnp.sum(sp, axis=-1, keepdims=True), o) for sp, o in zip(sps, outs)]

    def step(plan, carry):
        kv = {}
        items = []
        for kb, c, kind in plan:
            h = c // nr
            if (id(kb), h) not in kv:
                rows = pl.ds(pl.multiple_of(kb * kb_rows, kb_rows), kb_rows)
                kv[(id(kb), h)] = (k_ref[rows, lanes[h]].astype(BF16), v_ref[rows, lanes[h]].astype(BF16))
            k, v = kv[(id(kb), h)]
            items.append((qs[h][c % nr], k, v, strict if kind == "diag" else None))
        carry = list(carry)
        for (_, c, _), (rsum, o) in zip(plan, stages(items)):
            run, acc = carry[c]
            carry[c] = (run + rsum, acc + jnp.exp2(-run) * o)
        return tuple(carry)

    nch = nh * nr
    carry = tuple((jnp.zeros((kb_rows, 1), F32), jnp.zeros((kb_rows, hd), F32)) for _ in range(nch))
    first = [qi * nr + d for d in range(nr)]
    plan = [(first[d], h * nr + r, "diag" if r == d else "full")
            for d in reversed(range(nr)) for h in range(nh) for r in range(d, nr)]
    carry = step(plan, carry)
    nfull = qi * nr
    per_it = 2 if nr % 2 == 0 else 1

    def body(j, c):
        kbs = [nfull - 1 - per_it * j - u for u in range(per_it)]
        return step([(kb, ch, "full") for kb in kbs for ch in range(nch)], c)

    carry = lax.fori_loop(0, nfull // per_it, body, carry)
    for h in range(nh):
        for r in range(nr):
            o_ref[r * kb_rows:(r + 1) * kb_rows, lanes[h]] = carry[h * nr + r][1].astype(o_ref.dtype)


def _mixer_d(proj, batch, seq):
    t = proj.shape[0]
    tq = min(2 * SB_KEYS, seq)
    nq = seq // tq
    pairs = GROUP_W // LANE
    return pl.pallas_call(
        _sb_kernel,
        grid=(batch, pairs, nq),
        in_specs=[
            pl.BlockSpec((tq, LANE), lambda b, p, i: (b * nq + i, U_DQ + p)),
            pl.BlockSpec((seq, LANE), lambda b, p, i: (b, U_DK + p)),
            pl.BlockSpec((seq, LANE), lambda b, p, i: (b, U_DV + p)),
        ],
        out_specs=pl.BlockSpec((tq, LANE), lambda b, p, i: (b * nq + i, p)),
        out_shape=jax.ShapeDtypeStruct((t, GROUP_W), BF16),
        compiler_params=_cparams(("arbitrary", "arbitrary", "arbitrary")),
        name="mixer_d_stickbreaking",
    )(proj, proj, proj)


def _outproj_kernel(ya_ref, yb_ref, yc_ref, yd_ref, gnd_ref, w_ref, x_ref, gate_ref, sh_ref, sc_ref, g_ref,
                    rw_ref, xo_ref, h_ref, lg_ref):
    w = GROUP_W
    yd = _rms(yd_ref[...].astype(F32), gnd_ref[...]).astype(BF16)
    mix = (_dot(ya_ref[...], w_ref[0:w, :]) + _dot(yb_ref[...], w_ref[w:2 * w, :])
           + _dot(yc_ref[...], w_ref[2 * w:3 * w, :]) + _dot(yd, w_ref[3 * w:4 * w, :]))
    x = x_ref[...] + gate_ref[0] * mix
    xo_ref[...] = x
    h = _rms(x, g_ref[...]) * (1.0 + sc_ref[0]) + sh_ref[0]
    for c in range(ROW_CHUNKS):
        h_ref[pl.ds(c, h.shape[0], stride=ROW_CHUNKS), :] = h[:, c * LANE:(c + 1) * LANE]
    lg_ref[...] = _dot3(h, rw_ref[...]).T[0:N_EXPERTS, :]


def _out_projection(ya, yb, yc, yd, gn_d, w_out_b, layer, x2, mod6, g2, router_w_p, seq):
    t, d = x2.shape
    tm = min(256, seq)
    per_b = seq // tm
    row = lambda i: (i, 0)
    const = lambda i: (0, 0)
    modspec = lambda m: pl.BlockSpec((1, 1, d), lambda i: ((i // per_b) * N_MOD + m, 0, 0))
    return pl.pallas_call(
        _outproj_kernel,
        grid=(t // tm,),
        in_specs=[
            pl.BlockSpec((tm, GROUP_W), row), pl.BlockSpec((tm, GROUP_W), row),
            pl.BlockSpec((tm, GROUP_W), row), pl.BlockSpec((tm, GROUP_W), row),
            pl.BlockSpec((1, GROUP_W), const),
            pl.BlockSpec((None, d, d), lambda i: (layer, 0, 0)),
            pl.BlockSpec((tm, d), row),
            modspec(2), modspec(3), modspec(4),
            pl.BlockSpec((1, d), const),
            pl.BlockSpec((d, LANE), const),
        ],
        out_specs=[
            pl.BlockSpec((tm, d), row),
            pl.BlockSpec((tm * ROW_CHUNKS, LANE), row),
            pl.BlockSpec((N_EXPERTS, tm), lambda i: (0, i)),
        ],
        out_shape=[
            jax.ShapeDtypeStruct((t, d), F32),
            jax.ShapeDtypeStruct((t * ROW_CHUNKS, LANE), F32),
            jax.ShapeDtypeStruct((N_EXPERTS, t), F32),
        ],
        compiler_params=_cparams(("arbitrary",)),
        name="outproj_adaln_router",
    )(ya, yb, yc, yd, gn_d.reshape(1, GROUP_W), w_out_b, x2, mod6, mod6, mod6, g2.reshape(1, d), router_w_p)


def _route_kernel(lg_ref, bias_ref, oi_ref, of_ref, cnt_ref):
    ne = N_EXPERTS
    rt = lg_ref.shape[1]

    @pl.when(pl.program_id(0) == 0)
    def _():
        cnt_ref[...] = jnp.zeros_like(cnt_ref)

    lg = lg_ref[...]
    e = jnp.exp(lg - jnp.max(lg, axis=0, keepdims=True))
    probs = e / jnp.sum(e, axis=0, keepdims=True)
    sel = probs + bias_ref[...][:, 0:1]
    rowi = lax.broadcasted_iota(jnp.int32, (ne, rt), 0)
    neg = -jnp.inf

    gscore = []
    for g in range(N_EXPERT_GROUPS):
        r = [sel[g * EXPERTS_PER_GROUP + i:g * EXPERTS_PER_GROUP + i + 1, :] for i in range(EXPERTS_PER_GROUP)]
        best = None
        for i in range(EXPERTS_PER_GROUP):
            for j in range(i + 1, EXPERTS_PER_GROUP):
                pair = jnp.maximum(r[i], r[j]) + jnp.minimum(r[i], r[j])
                best = pair if best is None else jnp.maximum(best, pair)
        gscore.append(best)
    top_g = jnp.zeros((1, rt), jnp.int32)
    top_s = gscore[0]
    for g in range(1, N_EXPERT_GROUPS):
        better = gscore[g] > top_s
        top_g = jnp.where(better, g, top_g)
        top_s = jnp.where(better, gscore[g], top_s)

    in_grp = (rowi // EXPERTS_PER_GROUP) == top_g
    masked = jnp.where(in_grp, sel, neg)
    m1 = jnp.max(masked, axis=0, keepdims=True)
    i1 = jnp.min(jnp.where(masked == m1, rowi, ne), axis=0, keepdims=True)
    masked2 = jnp.where(rowi == i1, neg, masked)
    m2 = jnp.max(masked2, axis=0, keepdims=True)
    i2 = jnp.min(jnp.where(masked2 == m2, rowi, ne), axis=0, keepdims=True)
    hit1 = rowi == i1
    hit2 = rowi == i2
    g1 = jnp.sum(jnp.where(hit1, probs, 0.0), axis=0, keepdims=True)
    g2 = jnp.sum(jnp.where(hit2, probs, 0.0), axis=0, keepdims=True)
    gsum = g1 + g2

    onehot = jnp.where(hit1 | hit2, 1.0, 0.0)
    ji = lax.broadcasted_iota(jnp.int32, (rt, rt), 0)
    ti = lax.broadcasted_iota(jnp.int32, (rt, rt), 1)
    before = jnp.where(ji < ti, 1.0, 0.0).astype(BF16)
    base = cnt_ref[...][:, 0:1]
    prior = _dot(onehot.astype(BF16), before) + base
    r1 = jnp.sum(jnp.where(hit1, prior, 0.0), axis=0, keepdims=True)
    r2 = jnp.sum(jnp.where(hit2, prior, 0.0), axis=0, keepdims=True)
    cnt_ref[...] = cnt_ref[...] + jnp.sum(onehot, axis=1, keepdims=True)

    zi = jnp.zeros((4, rt), jnp.int32)
    oi_ref[...] = jnp.concatenate([i1, i2, r1.astype(jnp.int32), r2.astype(jnp.int32), zi], axis=0)
    zf = jnp.zeros((6, rt), F32)
    of_ref[...] = jnp.concatenate([g1 / gsum, g2 / gsum, zf], axis=0)


def _routing(logits_t, router_bias):
    ne, t = logits_t.shape
    rt = min(512, t)
    bias = jnp.broadcast_to(router_bias.astype(F32)[:, None], (ne, LANE))
    return pl.pallas_call(
        _route_kernel,
        grid=(t // rt,),
        in_specs=[pl.BlockSpec((ne, rt), lambda i: (0, i)), pl.BlockSpec((ne, LANE), lambda i: (0, 0))],
        out_specs=[
            pl.BlockSpec((8, rt), lambda i: (0, i)),
            pl.BlockSpec((8, rt), lambda i: (0, i)),
            pl.BlockSpec((ne, LANE), lambda i: (0, 0)),
        ],
        out_shape=[
            jax.ShapeDtypeStruct((8, t), jnp.int32),
            jax.ShapeDtypeStruct((8, t), F32),
            jax.ShapeDtypeStruct((ne, LANE), F32),
        ],
        compiler_params=_cparams(("arbitrary",)),
        name="moe_routing",
    )(logits_t, bias)


def _moe_kernel(blk_e, src_tok, dst_row, h_hbm, wg_ref, wu_ref, wd_ref, out_hbm, xbuf, ybuf, gsem, ssem):
    i = pl.program_id(0)
    nblk = pl.num_programs(0)
    rc = ROW_CHUNKS
    rb = xbuf.shape[1] // rc
    slot = lax.rem(i, MOE_SLOTS)
    slot_next2 = lax.rem(i + 2, MOE_SLOTS)

    def gather_start(plan_blk, s, j):
        tok = pl.multiple_of(src_tok[plan_blk * rb + j] * rc, rc)
        pltpu.make_async_copy(h_hbm.at[pl.ds(tok, rc), :], xbuf.at[s, pl.ds(j * rc, rc), :], gsem.at[s]).start()

    def scatter_start(plan_blk, s, j):
        row = pl.multiple_of(dst_row[plan_blk * rb + j] * rc, rc)
        pltpu.make_async_copy(ybuf.at[s, pl.ds(j * rc, rc), :], out_hbm.at[pl.ds(row, rc), :], ssem.at[s]).start()

    def gather_wait(s):
        pltpu.make_async_copy(h_hbm.at[pl.ds(0, rb * rc), :], xbuf.at[s], gsem.at[s]).wait()

    def scatter_wait(s):
        pltpu.make_async_copy(ybuf.at[s], out_hbm.at[pl.ds(0, rb * rc), :], ssem.at[s]).wait()

    @pl.when(i == 0)
    def _():
        ybuf[...] = jnp.zeros_like(ybuf)
        for j in range(rb):
            gather_start(1, 0, j)
            gather_start(2, 1, j)

    gather_wait(slot)

    @pl.when(i >= 2)
    def _():
        scatter_wait(slot)

    x = jnp.concatenate([xbuf[slot, pl.ds(c, rb, stride=rc), :] for c in range(rc)], axis=1).astype(BF16)
    for j in range(rb):
        gather_start(i + 3, slot_next2, j)
        scatter_start(i, slot_next2, j)
    hid = _silu(_dot(x, wg_ref[0])) * _dot(x, wu_ref[0])
    y = _dot(hid.astype(BF16), wd_ref[0])
    for c in range(rc):
        ybuf[slot, pl.ds(c, rb, stride=rc), :] = y[:, c * LANE:(c + 1) * LANE]

    @pl.when(i == nblk - 1)
    def _():
        for j in range(rb):
            scatter_start(i + 1, slot, j)
        for s in range(MOE_SLOTS):
            scatter_wait(s)
        gather_wait(lax.rem(i + 1, MOE_SLOTS))
        gather_wait(slot_next2)


def _experts(blk_e, src_tok, dst_row, h2, wg, wu, wd, layer, rb):
    rc = ROW_CHUNKS
    _, ne, d, dff = wg.shape
    nblk = blk_e.shape[0]
    out_rows = (nblk + 1) * rb
    return pl.pallas_call(
        _moe_kernel,
        grid_spec=pltpu.PrefetchScalarGridSpec(
            num_scalar_prefetch=3,
            grid=(nblk,),
            in_specs=[
                pl.BlockSpec(memory_space=pl.ANY),
                pl.BlockSpec((None, 1, d, dff), lambda i, be, st, dr: (layer, be[i], 0, 0)),
                pl.BlockSpec((None, 1, d, dff), lambda i, be, st, dr: (layer, be[i], 0, 0)),
                pl.BlockSpec((None, 1, dff, d), lambda i, be, st, dr: (layer, be[i], 0, 0)),
            ],
            out_specs=pl.BlockSpec(memory_space=pl.ANY),
            scratch_shapes=[
                pltpu.VMEM((MOE_SLOTS, rb * rc, LANE), F32),
                pltpu.VMEM((MOE_SLOTS, rb * rc, LANE), F32),
                pltpu.SemaphoreType.DMA((MOE_SLOTS,)),
                pltpu.SemaphoreType.DMA((MOE_SLOTS,)),
            ],
        ),
        out_shape=jax.ShapeDtypeStruct((out_rows * rc, LANE), F32),
        compiler_params=_cparams(("arbitrary",)),
        name="moe_experts",
    )(blk_e, src_tok, dst_row, h2, wg, wu, wd)


def _combine_kernel(x_ref, y0_ref, y1_ref, w_ref, gate_ref, g_ref, o_ref, *, final):
    w = w_ref[...]
    tm = x_ref.shape[0]
    rows = lambda ref: jnp.concatenate([ref[pl.ds(c, tm, stride=ROW_CHUNKS), :] for c in range(ROW_CHUNKS)], axis=1)
    moe = w[:, 0:1] * rows(y0_ref) + w[:, 1:2] * rows(y1_ref)
    x = x_ref[...] + gate_ref[0] * moe
    if final:
        x = _rms(x, g_ref[...])
    o_ref[...] = x


def _combine(x2, y2, gates_t, mod6, final_g, seq, final):
    t, d = x2.shape
    tm = min(512, seq)
    per_b = seq // tm
    nt = t // tm
    return pl.pallas_call(
        functools.partial(_combine_kernel, final=final),
        grid=(nt,),
        in_specs=[
            pl.BlockSpec((tm, d), lambda i: (i, 0)),
            pl.BlockSpec((tm * ROW_CHUNKS, LANE), lambda i: (i, 0)),
            pl.BlockSpec((tm * ROW_CHUNKS, LANE), lambda i: (nt + i, 0)),
            pl.BlockSpec((tm, 2), lambda i: (i, 0)),
            pl.BlockSpec((1, 1, d), lambda i: ((i // per_b) * N_MOD + 5, 0, 0)),
            pl.BlockSpec((1, d), lambda i: (0, 0)),
        ],
        out_specs=pl.BlockSpec((tm, d), lambda i: (i, 0)),
        out_shape=jax.ShapeDtypeStruct((t, d), F32),
        compiler_params=_cparams(("arbitrary",)),
        name="moe_combine",
    )(x2, y2, y2, gates_t, mod6, final_g.reshape(1, d))


def _permute_w_in(w_in):
    depth, d, _ = w_in.shape
    gw = GROUP_W
    kv = SWA_KV_HEADS * HEAD_DIM
    xbc = gw + 2 * SSM_GROUPS * SSM_STATE
    o = 0
    a = w_in[:, :, o:o + 3 * gw]; o += 3 * gw
    bq = w_in[:, :, o:o + gw]; o += gw
    bk = w_in[:, :, o:o + kv]; o += kv
    bv = w_in[:, :, o:o + kv]; o += kv
    cz = w_in[:, :, o:o + gw]; o += gw
    cx = w_in[:, :, o:o + xbc]; o += xbc
    tail = w_in[:, :, o:]
    cdt = tail[:, :, 0:N_HEADS]
    dqkv = tail[:, :, N_HEADS:N_HEADS + 3 * gw]
    zeros = lambda n: jnp.zeros((depth, d, n), w_in.dtype)
    out = jnp.concatenate([a, dqkv, cz, bq, cx, bk, bv, cdt, zeros(LANE - N_HEADS), zeros(LANE)], axis=-1)
    return out.astype(BF16)


def _dispatch_plan(idx, cnt, t, rb):
    ne = N_EXPERTS
    counts = cnt[:, 0].astype(jnp.int32)
    padded = (counts + rb - 1) // rb * rb
    pad_ends = jnp.cumsum(padded)
    pad_starts = pad_ends - padded
    e_idx = idx[0:2]
    rank = idx[2:4]
    onehot = e_idx[:, :, None] == jnp.arange(ne, dtype=jnp.int32)
    dest = jnp.sum(jnp.where(onehot, pad_starts, 0), axis=-1) + rank + rb
    nblk = (2 * t + ne * rb) // rb
    npos = (nblk + 3) * rb
    tok = jnp.broadcast_to(jnp.arange(t, dtype=jnp.int32)[None, :], (2, t))
    out_row = jnp.arange(2, dtype=jnp.int32)[:, None] * t + tok
    placed = jnp.full((npos,), -1, jnp.int32).at[dest.reshape(-1)].set(out_row.reshape(-1))
    is_pad = placed < 0
    dump_row = 2 * t + jnp.cumsum(is_pad.astype(jnp.int32)) - 1
    dst_row = jnp.where(is_pad, dump_row, placed)
    src_tok = jnp.where(is_pad, 0, jnp.where(placed >= t, placed - t, placed))
    blk_start = jnp.arange(nblk, dtype=jnp.int32) * rb
    blk_e = jnp.minimum(jnp.sum(pad_ends[None, :] <= blk_start[:, None], axis=1), ne - 1).astype(jnp.int32)
    return blk_e, src_tok, dst_row


def kernel(x, c, w_mod, b_mod, norm1_g, norm2_g, w_in, w_out, conv_a_w, gn_a, attn_sinks, gn_b, ssm_conv_w,
           ssm_conv_b, dt_bias, a_log, d_skip, ssm_norm_g, gn_d, router_w, router_bias, moe_w_gate, moe_w_up,
           moe_w_down, final_g):
    batch, seq, d = x.shape
    assert d == ROW_CHUNKS * LANE
    depth = w_in.shape[0]
    t = batch * seq
    rb = 256

    mod = _modulation(c, w_mod, b_mod)
    w_in_p = _permute_w_in(w_in)
    w_out_b = w_out.astype(BF16)
    wg_b, wu_b, wd_b = moe_w_gate.astype(BF16), moe_w_up.astype(BF16), moe_w_down.astype(BF16)
    router_w_p = jnp.zeros((d, LANE), F32).at[:, :N_EXPERTS].set(router_w)

    x2 = x.reshape(t, d)
    for l in range(depth):
        mod6 = mod[l].reshape(batch * N_MOD, 1, d)
        proj = _in_projection(x2, norm1_g[l], mod6, w_in_p, l, seq)
        ya = _mixer_a(proj, conv_a_w[l], gn_a[l], batch, seq)
        yb = _mixer_b(proj, attn_sinks[l], gn_b[l], batch, seq)
        yc = _mixer_c(proj, ssm_conv_w[l], ssm_conv_b[l], dt_bias[l], a_log[l], d_skip[l], ssm_norm_g[l],
                      batch, seq)
        yd = _mixer_d(proj, batch, seq)
        x2, h2, logits_t = _out_projection(ya, yb, yc, yd, gn_d[l], w_out_b, l, x2, mod6, norm2_g[l],
                                           router_w_p, seq)
        idx, gates, cnt = _routing(logits_t, router_bias)
        blk_e, src_tok, dst_row = _dispatch_plan(idx, cnt, t, rb)
        y2 = _experts(blk_e, src_tok, dst_row, h2, wg_b, wu_b, wd_b, l, rb)
        x2 = _combine(x2, y2, gates[0:2].T, mod6, final_g, seq, final=(l == depth - 1))
    return x2.reshape(batch, seq, d)
```

```python
import functools
import math

import jax
import jax.numpy as jnp
from jax import lax
from jax.experimental import pallas as pl
from jax.experimental.pallas import tpu as pltpu

F32 = jnp.float32
BF16 = jnp.bfloat16

EPS = 1e-6
HEAD_DIM = 64
GROUP_W = 512
N_HEADS = GROUP_W // HEAD_DIM
SWA_KV_HEADS = 2
SWA_WINDOW = 128
SSM_GROUPS = 2
SSM_STATE = 128
SSM_CHUNK = 128
N_EXPERTS = 16
N_EXPERT_GROUPS = 4
EXPERTS_PER_GROUP = N_EXPERTS // N_EXPERT_GROUPS
N_MOD = 6

LANE = 128
HALO = 16
SB_KEYS = 256
ROW_CHUNKS = 16
MOE_SLOTS = 3
VMEM_LIMIT = 56 * 1024 * 1024

U_A = 0
U_DQ = 12
U_DK = 16
U_DV = 20
U_Z = 24
U_BQ = 28
U_XBC = 32
U_BK = 40
U_BV = 41
U_DT = 42
N_UNITS = 44
NP = N_UNITS * LANE

PROJ_DTYPE = jnp.bfloat16


def _cparams(sem, vmem=VMEM_LIMIT):
    return pltpu.CompilerParams(dimension_semantics=sem, vmem_limit_bytes=vmem)


def _rms(y, g):
    return y * lax.rsqrt(jnp.mean(y * y, axis=-1, keepdims=True) + EPS) * g


def _split_bf16(a):
    hi = a.astype(BF16)
    lo = (a - hi.astype(F32)).astype(BF16)
    return hi, lo


def _dot(a, b):
    return jnp.dot(a, b, preferred_element_type=F32)


def _dot3(a, b):
    ah, al = _split_bf16(a)
    bh, bl = _split_bf16(b)
    return _dot(ah, bh) + _dot(al, bh) + _dot(ah, bl)


def _silu(a):
    return a * (1.0 / (1.0 + jnp.exp(-a)))


def _softplus(a):
    return jnp.maximum(a, 0.0) + jnp.log1p(jnp.exp(-jnp.abs(a)))


def _mod_kernel(c_ref, w_ref, b_ref, o_ref):
    cond = _silu(c_ref[...])
    o_ref[0] = _dot3(cond, w_ref[0]) + b_ref[0]


def _modulation(c, w_mod, b_mod):
    depth, d, nm = w_mod.shape
    b = c.shape[0]
    rows = 8
    cp = jnp.zeros((rows, d), F32).at[:b].set(c)
    tn = 1024
    out = pl.pallas_call(
        _mod_kernel,
        grid=(depth, nm // tn),
        in_specs=[
            pl.BlockSpec((rows, d), lambda l, j: (0, 0)),
            pl.BlockSpec((1, d, tn), lambda l, j: (l, 0, j)),
            pl.BlockSpec((1, 1, tn), lambda l, j: (l, 0, j)),
        ],
        out_specs=pl.BlockSpec((1, rows, tn), lambda l, j: (l, 0, j)),
        out_shape=jax.ShapeDtypeStruct((depth, rows, nm), F32),
        compiler_params=_cparams(("arbitrary", "arbitrary")),
        name="modulation",
    )(cp, w_mod, b_mod.reshape(depth, 1, nm))
    return out[:, :b]


def _inproj_kernel(x_ref, g_ref, sh_ref, sc_ref, w_ref, o_ref, h_scr):
    @pl.when(pl.program_id(1) == 0)
    def _():
        chunk = LANE

        def body(r, carry):
            rows = pl.ds(pl.multiple_of(r * chunk, chunk), chunk)
            h = _rms(x_ref[rows, :], g_ref[...]) * (1.0 + sc_ref[0]) + sh_ref[0]
            h_scr[rows, :] = h.astype(BF16)
            return carry

        lax.fori_loop(0, x_ref.shape[0] // chunk, body, 0)

    o_ref[...] = _dot(h_scr[...], w_ref[...]).astype(o_ref.dtype)


def _in_projection(x2, g, mod6, w_in_p, layer, seq):
    t, d = x2.shape
    tm = min(1024, seq)
    tn = 512
    per_b = seq // tm
    return pl.pallas_call(
        _inproj_kernel,
        grid=(t // tm, NP // tn),
        in_specs=[
            pl.BlockSpec((tm, d), lambda i, j: (i, 0)),
            pl.BlockSpec((1, d), lambda i, j: (0, 0)),
            pl.BlockSpec((1, 1, d), lambda i, j: ((i // per_b) * N_MOD + 0, 0, 0)),
            pl.BlockSpec((1, 1, d), lambda i, j: ((i // per_b) * N_MOD + 1, 0, 0)),
            pl.BlockSpec((None, d, tn), lambda i, j: (layer, 0, j)),
        ],
        out_specs=pl.BlockSpec((tm, tn), lambda i, j: (i, j)),
        out_shape=jax.ShapeDtypeStruct((t, NP), PROJ_DTYPE),
        scratch_shapes=[pltpu.VMEM((tm, d), BF16)],
        compiler_params=_cparams(("arbitrary", "arbitrary")),
        name="adaln_inproj",
    )(x2, g.reshape(1, d), mod6, mod6, w_in_p)


def _mixa_kernel(p_ref, halo_ref, w_ref, g_ref, o_ref, ext_scr):
    ts = p_ref.shape[0]
    w = GROUP_W
    p = p_ref[...].astype(F32)
    hp = halo_ref[...].astype(F32)
    u = p[:, w:2 * w] * p[:, 2 * w:3 * w]
    hu = hp[:, w:2 * w] * hp[:, 2 * w:3 * w]
    hu = jnp.where(pl.program_id(1) == 0, 0.0, hu)
    ext_scr[0:HALO, :] = hu
    ext_scr[HALO:HALO + ts, :] = u
    cw = w_ref[...]
    conv = (cw[0:1] * ext_scr[HALO - 2:HALO - 2 + ts, :]
            + cw[1:2] * ext_scr[HALO - 1:HALO - 1 + ts, :]
            + cw[2:3] * u)
    o_ref[...] = _rms(p[:, 0:w] * conv, g_ref[...]).astype(o_ref.dtype)


def _mixer_a(proj, conv_w, gn, batch, seq):
    t = proj.shape[0]
    ts = min(512, seq)
    nt = seq // ts
    hb = ts // HALO
    wa = 3 * GROUP_W
    return pl.pallas_call(
        _mixa_kernel,
        grid=(batch, nt),
        in_specs=[
            pl.BlockSpec((ts, wa), lambda b, i: (b * nt + i, U_A * LANE // wa)),
            pl.BlockSpec((HALO, wa), lambda b, i: (jnp.maximum((b * nt + i) * hb - 1, 0), U_A * LANE // wa)),
            pl.BlockSpec(conv_w.shape, lambda b, i: (0, 0)),
            pl.BlockSpec((1, GROUP_W), lambda b, i: (0, 0)),
        ],
        out_specs=pl.BlockSpec((ts, GROUP_W), lambda b, i: (b * nt + i, 0)),
        out_shape=jax.ShapeDtypeStruct((t, GROUP_W), BF16),
        scratch_shapes=[pltpu.VMEM((HALO + ts, GROUP_W), F32)],
        compiler_params=_cparams(("arbitrary", "arbitrary")),
        name="mixer_a_conv",
    )(proj, proj, conv_w, gn.reshape(1, GROUP_W))


def _swa_kernel(q_ref, k_ref, kp_ref, v_ref, vp_ref, sink_ref, slope_ref, g_ref, o_ref, y_scr):
    w = SWA_WINDOW
    hd = HEAD_DIM
    grp = N_HEADS // SWA_KV_HEADS
    nblk = q_ref.shape[0] // w
    first = pl.program_id(1) == 0
    q = (q_ref[...].astype(F32) * (1.0 / math.sqrt(hd))).astype(BF16)
    k = jnp.concatenate([kp_ref[...], k_ref[...]], axis=0).astype(BF16)
    v = jnp.concatenate([vp_ref[...], v_ref[...]], axis=0).astype(BF16)
    rows = grp * w
    qi = lax.broadcasted_iota(jnp.int32, (rows, 2 * w), 0) % w
    kj = lax.broadcasted_iota(jnp.int32, (rows, 2 * w), 1)
    hrow = lax.broadcasted_iota(jnp.int32, (rows, 1), 0) // w
    dist = qi + w - kj
    valid = (dist >= 0) & (dist < w)
    distf = dist.astype(F32)
    sinks = sink_ref[...]
    slopes = slope_ref[...]

    chains = []
    for kvh in range(SWA_KV_HEADS):
        slope = jnp.zeros((rows, 1), F32)
        sink = jnp.zeros((rows, 1), F32)
        for gi in range(grp):
            h = kvh * grp + gi
            slope = jnp.where(hrow == gi, slopes[0:1, h:h + 1], slope)
            sink = jnp.where(hrow == gi, sinks[0:1, h:h + 1], sink)
        bias = jnp.where(valid, -slope * distf, -jnp.inf)
        bias0 = jnp.where(first & (kj < w), -jnp.inf, bias)
        for n in range(nblk):
            qs = jnp.concatenate([q[n * w:(n + 1) * w, (kvh * grp + gi) * hd:(kvh * grp + gi + 1) * hd]
                                  for gi in range(grp)], axis=0)
            kn = k[n * w:(n + 2) * w, kvh * hd:(kvh + 1) * hd]
            vn = v[n * w:(n + 2) * w, kvh * hd:(kvh + 1) * hd]
            chains.append((kvh, n, qs, kn, vn, bias0 if n == 0 else bias, sink))

    scores = [lax.dot_general(qs, kn, (((1,), (1,)), ((), ())), preferred_element_type=F32) + bias
              for _, _, qs, kn, _, bias, _ in chains]
    es, rden = [], []
    for (_, _, _, _, _, _, sink), s in zip(chains, scores):
        m = jnp.maximum(jnp.max(s, axis=-1, keepdims=True), sink)
        e = jnp.exp(s - m)
        rden.append(1.0 / (jnp.sum(e, axis=-1, keepdims=True) + jnp.exp(sink - m)))
        es.append(e.astype(BF16))
    outs = [_dot(e, vn) * r for (_, _, _, _, vn, _, _), e, r in zip(chains, es, rden)]
    for (kvh, n, _, _, _, _, _), o in zip(chains, outs):
        for gi in range(grp):
            h = kvh * grp + gi
            y_scr[n * w:(n + 1) * w, h * hd:(h + 1) * hd] = o[gi * w:(gi + 1) * w, :]
    o_ref[...] = _rms(y_scr[...], g_ref[...]).astype(o_ref.dtype)


def _mixer_b(proj, sinks, gn, batch, seq):
    t = proj.shape[0]
    w = SWA_WINDOW
    ts = min(512, seq)
    nt = seq // ts
    pb = ts // w
    slopes = 2.0 ** (-8.0 * jnp.arange(1, N_HEADS + 1, dtype=F32) / N_HEADS)
    pad = lambda a: jnp.zeros((1, LANE), F32).at[0, :N_HEADS].set(a)
    cur = lambda u: (lambda b, i: (b * nt + i, u))
    prev = lambda u: (lambda b, i: (jnp.maximum((b * nt + i) * pb - 1, 0), u))
    return pl.pallas_call(
        _swa_kernel,
        grid=(batch, nt),
        in_specs=[
            pl.BlockSpec((ts, GROUP_W), cur(U_BQ * LANE // GROUP_W)),
            pl.BlockSpec((ts, LANE), cur(U_BK)),
            pl.BlockSpec((w, LANE), prev(U_BK)),
            pl.BlockSpec((ts, LANE), cur(U_BV)),
            pl.BlockSpec((w, LANE), prev(U_BV)),
            pl.BlockSpec((1, LANE), lambda b, i: (0, 0)),
            pl.BlockSpec((1, LANE), lambda b, i: (0, 0)),
            pl.BlockSpec((1, GROUP_W), lambda b, i: (0, 0)),
        ],
        out_specs=pl.BlockSpec((ts, GROUP_W), lambda b, i: (b * nt + i, 0)),
        out_shape=jax.ShapeDtypeStruct((t, GROUP_W), BF16),
        scratch_shapes=[pltpu.VMEM((ts, GROUP_W), F32)],
        compiler_params=_cparams(("arbitrary", "arbitrary")),
        name="mixer_b_swa",
    )(proj, proj, proj, proj, proj, pad(sinks), pad(slopes), gn.reshape(1, GROUP_W))


def _ssd_kernel(z_ref, xbc_ref, halo_ref, dt_ref, cw_ref, cb_ref, dtb_ref, alog_ref, dsk_ref, g_ref,
                o_ref, ext_scr, y_scr, off_scr, state_scr):
    l = SSM_CHUNK
    hd = HEAD_DIM
    n = SSM_STATE
    rep = N_HEADS // SSM_GROUPS
    first = pl.program_id(1) == 0

    @pl.when(first)
    def _():
        state_scr[...] = jnp.zeros_like(state_scr)

    raw = xbc_ref[...].astype(F32)
    halo = jnp.where(first, 0.0, halo_ref[...].astype(F32))
    ext_scr[0:HALO, :] = halo
    ext_scr[HALO:HALO + l, :] = raw
    cw = cw_ref[...]
    conv = (cw[0:1] * ext_scr[HALO - 3:HALO - 3 + l, :]
            + cw[1:2] * ext_scr[HALO - 2:HALO - 2 + l, :]
            + cw[2:3] * ext_scr[HALO - 1:HALO - 1 + l, :]
            + cw[3:4] * raw)
    xbc = _silu(conv + cb_ref[...])
    xs = xbc[:, 0:GROUP_W]
    bm = xbc[:, GROUP_W:GROUP_W + SSM_GROUPS * n]
    cm = xbc[:, GROUP_W + SSM_GROUPS * n:GROUP_W + 2 * SSM_GROUPS * n]

    dt = _softplus(dt_ref[...].astype(F32) + dtb_ref[...])
    da = dt * (-jnp.exp(alog_ref[...]))
    ri = lax.broadcasted_iota(jnp.int32, (l, l), 0)
    ci = lax.broadcasted_iota(jnp.int32, (l, l), 1)
    causal = ri >= ci
    tril = jnp.where(causal, 1.0, 0.0).astype(BF16)
    da_hi, da_lo = _split_bf16(da)
    cum = _dot(tril, da_hi) + _dot(tril, da_lo)
    cum_t = cum.T
    last = cum[l - 1:l, :]
    er = lax.broadcasted_iota(jnp.int32, (LANE, GROUP_W), 0)
    ec = lax.broadcasted_iota(jnp.int32, (LANE, GROUP_W), 1)
    spread = jnp.where(ec // hd == er, 1.0, 0.0).astype(BF16)

    def per_head_lanes(a):
        a_hi, a_lo = _split_bf16(a)
        return _dot(a_hi, spread) + _dot(a_lo, spread)

    xdt_all = xs * per_head_lanes(dt)
    xend_all = xdt_all * per_head_lanes(jnp.exp(last - cum))
    keep_prev = jnp.exp(last)

    for gi in range(SSM_GROUPS):
        bg = bm[:, gi * n:(gi + 1) * n]
        cg = cm[:, gi * n:(gi + 1) * n].astype(BF16)
        bg_t = bg.T.astype(BF16)
        cb = _dot(cg, bg_t)
        for hi_ in range(rep):
            h = gi * rep + hi_
            lanes = slice(h * hd, (h + 1) * hd)
            col = cum[:, h:h + 1]
            row = cum_t[h:h + 1, :]
            decay = jnp.exp(jnp.where(causal, col - row, -jnp.inf))
            y_scr[:, lanes] = _dot((cb * decay).astype(BF16), xdt_all[:, lanes].astype(BF16))
            prev = state_scr[h]
            off_scr[:, lanes] = _dot(cg, prev.astype(BF16))
            new = _dot(bg_t, xend_all[:, lanes].astype(BF16))
            state_scr[h] = prev * keep_prev[0:1, h:h + 1] + new

    y = y_scr[...] + off_scr[...] * per_head_lanes(jnp.exp(cum)) + xs * dsk_ref[...]
    y = y * _silu(z_ref[...].astype(F32))
    o_ref[...] = _rms(y, g_ref[...]).astype(o_ref.dtype)


def _mixer_c(proj, conv_w, conv_b, dt_bias, a_log, d_skip, norm_g, batch, seq):
    t = proj.shape[0]
    l = SSM_CHUNK
    nc = seq // l
    wx = GROUP_W + 2 * SSM_GROUPS * SSM_STATE
    hb = l // HALO
    pad = lambda a: jnp.zeros((1, LANE), F32).at[0, :N_HEADS].set(a)
    cur = lambda u: (lambda b, c: (b * nc + c, u))
    return pl.pallas_call(
        _ssd_kernel,
        grid=(batch, nc),
        in_specs=[
            pl.BlockSpec((l, GROUP_W), cur(U_Z * LANE // GROUP_W)),
            pl.BlockSpec((l, wx), cur(U_XBC * LANE // wx)),
            pl.BlockSpec((HALO, wx), lambda b, c: (jnp.maximum((b * nc + c) * hb - 1, 0), U_XBC * LANE // wx)),
            pl.BlockSpec((l, LANE), cur(U_DT)),
            pl.BlockSpec(conv_w.shape, lambda b, c: (0, 0)),
            pl.BlockSpec((1, wx), lambda b, c: (0, 0)),
            pl.BlockSpec((1, LANE), lambda b, c: (0, 0)),
            pl.BlockSpec((1, LANE), lambda b, c: (0, 0)),
            pl.BlockSpec((1, GROUP_W), lambda b, c: (0, 0)),
            pl.BlockSpec((1, GROUP_W), lambda b, c: (0, 0)),
        ],
        out_specs=pl.BlockSpec((l, GROUP_W), lambda b, c: (b * nc + c, 0)),
        out_shape=jax.ShapeDtypeStruct((t, GROUP_W), BF16),
        scratch_shapes=[
            pltpu.VMEM((HALO + l, wx), F32),
            pltpu.VMEM((l, GROUP_W), F32),
            pltpu.VMEM((l, GROUP_W), F32),
            pltpu.VMEM((N_HEADS, SSM_STATE, HEAD_DIM), F32),
        ],
        compiler_params=_cparams(("arbitrary", "arbitrary")),
        name="mixer_c_ssd",
    )(proj, proj, proj, proj, conv_w, conv_b.reshape(1, wx), pad(dt_bias), pad(a_log),
      jnp.repeat(d_skip.astype(F32), HEAD_DIM).reshape(1, GROUP_W), norm_g.reshape(1, GROUP_W))


def _sb_kernel(q_ref, k_ref, v_ref, o_ref):
    kb_rows = SB_KEYS
    tq = q_ref.shape[0]
    nr = tq // kb_rows
    hd = HEAD_DIM
    nh = LANE // hd
    qi = pl.program_id(2)
    ri = lax.broadcasted_iota(jnp.int32, (kb_rows, kb_rows), 0)
    ci = lax.broadcasted_iota(jnp.int32, (kb_rows, kb_rows), 1)
    strict = ci < ri
    later = jnp.where(ri > ci, 1.0, 0.0).astype(BF16)
    lanes = [slice(h * hd, (h + 1) * hd) for h in range(nh)]
    qscale = math.log2(math.e) / math.sqrt(hd)
    qs = [[(q_ref[r * kb_rows:(r + 1) * kb_rows, ln].astype(F32) * qscale).astype(BF16) for r in range(nr)]
          for ln in lanes]
    sign = jnp.uint32(0x80000000)

    def stages(items):
        zs = [lax.dot_general(q, k, (((1,), (1,)), ((), ())), preferred_element_type=F32)
              for q, k, _, _ in items]
        sps = []
        for (_, _, _, keep), z in zip(items, zs):
            neg_abs = lax.bitcast_convert_type(lax.bitcast_convert_type(z, jnp.uint32) | sign, F32)
            sp = jnp.maximum(z, 0.0) + jnp.log(1.0 + jnp.exp2(neg_abs)) * math.log2(math.e)
            sps.append(sp if keep is None else jnp.where(keep, sp, 0.0))
        afters = [_dot(sp.astype(BF16), later) for sp in sps]
        probs = []
        for (_, _, _, keep), z, sp, after in zip(items, zs, sps, afters):
            a = jnp.exp2(z - sp - after)
            probs.append((a if keep is None else jnp.where(keep, a, 0.0)).astype(BF16))
        outs = [_dot(a, v) for (_, _, v, _), a in zip(items, probs)]
        return [(jnp.sum(sp, axis=-1, keepdims=True), o) for sp, o in zip(sps, outs)]

    def step(plan, carry):
        kv = {}
        items = []
        for kb, c, kind in plan:
            h = c // nr
            if (id(kb), h) not in kv:
                rows = pl.ds(pl.multiple_of(kb * kb_rows, kb_rows), kb_rows)
                kv[(id(kb), h)] = (k_ref[rows, lanes[h]].astype(BF16), v_ref[rows, lanes[h]].astype(BF16))
            k, v = kv[(id(kb), h)]
            items.append((qs[h][c % nr], k, v, strict if kind == "diag" else None))
        carry = list(carry)
        for (_, c, _), (rsum, o) in zip(plan, stages(items)):
            run, acc = carry[c]
            carry[c] = (run + rsum, acc + jnp.exp2(-run) * o)
        return tuple(carry)

    nch = nh * nr
    carry = tuple((jnp.zeros((kb_rows, 1), F32), jnp.zeros((kb_rows, hd), F32)) for _ in range(nch))
    first = [qi * nr + d for d in range(nr)]
    plan = [(first[d], h * nr + r, "diag" if r == d else "full")
            for d in reversed(range(nr)) for h in range(nh) for r in range(d, nr)]
    carry = step(plan, carry)
    nfull = qi * nr
    per_it = 2 if nr % 2 == 0 else 1

    def body(j, c):
        kbs = [nfull - 1 - per_it * j - u for u in range(per_it)]
        return step([(kb, ch, "full") for kb in kbs for ch in range(nch)], c)

    carry = lax.fori_loop(0, nfull // per_it, body, carry)
    for h in range(nh):
        for r in range(nr):
            o_ref[r * kb_rows:(r + 1) * kb_rows, lanes[h]] = carry[h * nr + r][1].astype(o_ref.dtype)


def _mixer_d(proj, batch, seq):
    t = proj.shape[0]
    tq = min(2 * SB_KEYS, seq)
    nq = seq // tq
    pairs = GROUP_W // LANE
    return pl.pallas_call(
        _sb_kernel,
        grid=(batch, pairs, nq),
        in_specs=[
            pl.BlockSpec((tq, LANE), lambda b, p, i: (b * nq + i, U_DQ + p)),
            pl.BlockSpec((seq, LANE), lambda b, p, i: (b, U_DK + p)),
            pl.BlockSpec((seq, LANE), lambda b, p, i: (b, U_DV + p)),
        ],
        out_specs=pl.BlockSpec((tq, LANE), lambda b, p, i: (b * nq + i, p)),
        out_shape=jax.ShapeDtypeStruct((t, GROUP_W), BF16),
        compiler_params=_cparams(("arbitrary", "arbitrary", "arbitrary")),
        name="mixer_d_stickbreaking",
    )(proj, proj, proj)


def _outproj_kernel(ya_ref, yb_ref, yc_ref, yd_ref, gnd_ref, w_ref, x_ref, gate_ref, sh_ref, sc_ref, g_ref,
                    rw_ref, xo_ref, h_ref, lg_ref):
    w = GROUP_W
    yd = _rms(yd_ref[...].astype(F32), gnd_ref[...]).astype(BF16)
    mix = (_dot(ya_ref[...], w_ref[0:w, :]) + _dot(yb_ref[...], w_ref[w:2 * w, :])
           + _dot(yc_ref[...], w_ref[2 * w:3 * w, :]) + _dot(yd, w_ref[3 * w:4 * w, :]))
    x = x_ref[...] + gate_ref[0] * mix
    xo_ref[...] = x
    h = _rms(x, g_ref[...]) * (1.0 + sc_ref[0]) + sh_ref[0]
    for c in range(ROW_CHUNKS):
        h_ref[pl.ds(c, h.shape[0], stride=ROW_CHUNKS), :] = h[:, c * LANE:(c + 1) * LANE]
    lg_ref[...] = _dot3(h, rw_ref[...]).T[0:N_EXPERTS, :]


def _out_projection(ya, yb, yc, yd, gn_d, w_out_b, layer, x2, mod6, g2, router_w_p, seq):
    t, d = x2.shape
    tm = min(256, seq)
    per_b = seq // tm
    row = lambda i: (i, 0)
    const = lambda i: (0, 0)
    modspec = lambda m: pl.BlockSpec((1, 1, d), lambda i: ((i // per_b) * N_MOD + m, 0, 0))
    return pl.pallas_call(
        _outproj_kernel,
        grid=(t // tm,),
        in_specs=[
            pl.BlockSpec((tm, GROUP_W), row), pl.BlockSpec((tm, GROUP_W), row),
            pl.BlockSpec((tm, GROUP_W), row), pl.BlockSpec((tm, GROUP_W), row),
            pl.BlockSpec((1, GROUP_W), const),
            pl.BlockSpec((None, d, d), lambda i: (layer, 0, 0)),
            pl.BlockSpec((tm, d), row),
            modspec(2), modspec(3), modspec(4),
            pl.BlockSpec((1, d), const),
            pl.BlockSpec((d, LANE), const),
        ],
        out_specs=[
            pl.BlockSpec((tm, d), row),
            pl.BlockSpec((tm * ROW_CHUNKS, LANE), row),
            pl.BlockSpec((N_EXPERTS, tm), lambda i: (0, i)),
        ],
        out_shape=[
            jax.ShapeDtypeStruct((t, d), F32),
            jax.ShapeDtypeStruct((t * ROW_CHUNKS, LANE), F32),
            jax.ShapeDtypeStruct((N_EXPERTS, t), F32),
        ],
        compiler_params=_cparams(("arbitrary",)),
        name="outproj_adaln_router",
    )(ya, yb, yc, yd, gn_d.reshape(1, GROUP_W), w_out_b, x2, mod6, mod6, mod6, g2.reshape(1, d), router_w_p)


def _route_kernel(lg_ref, bias_ref, oi_ref, of_ref, cnt_ref):
    ne = N_EXPERTS
    rt = lg_ref.shape[1]

    @pl.when(pl.program_id(0) == 0)
    def _():
        cnt_ref[...] = jnp.zeros_like(cnt_ref)

    lg = lg_ref[...]
    e = jnp.exp(lg - jnp.max(lg, axis=0, keepdims=True))
    probs = e / jnp.sum(e, axis=0, keepdims=True)
    sel = probs + bias_ref[...][:, 0:1]
    rowi = lax.broadcasted_iota(jnp.int32, (ne, rt), 0)
    neg = -jnp.inf

    gscore = []
    for g in range(N_EXPERT_GROUPS):
        r = [sel[g * EXPERTS_PER_GROUP + i:g * EXPERTS_PER_GROUP + i + 1, :] for i in range(EXPERTS_PER_GROUP)]
        best = None
        for i in range(EXPERTS_PER_GROUP):
            for j in range(i + 1, EXPERTS_PER_GROUP):
                pair = jnp.maximum(r[i], r[j]) + jnp.minimum(r[i], r[j])
                best = pair if best is None else jnp.maximum(best, pair)
        gscore.append(best)
    top_g = jnp.zeros((1, rt), jnp.int32)
    top_s = gscore[0]
    for g in range(1, N_EXPERT_GROUPS):
        better = gscore[g] > top_s
        top_g = jnp.where(better, g, top_g)
        top_s = jnp.where(better, gscore[g], top_s)

    in_grp = (rowi // EXPERTS_PER_GROUP) == top_g
    masked = jnp.where(in_grp, sel, neg)
    m1 = jnp.max(masked, axis=0, keepdims=True)
    i1 = jnp.min(jnp.where(masked == m1, rowi, ne), axis=0, keepdims=True)
    masked2 = jnp.where(rowi == i1, neg, masked)
    m2 = jnp.max(masked2, axis=0, keepdims=True)
    i2 = jnp.min(jnp.where(masked2 == m2, rowi, ne), axis=0, keepdims=True)
    hit1 = rowi == i1
    hit2 = rowi == i2
    g1 = jnp.sum(jnp.where(hit1, probs, 0.0), axis=0, keepdims=True)
    g2 = jnp.sum(jnp.where(hit2, probs, 0.0), axis=0, keepdims=True)
    gsum = g1 + g2

    onehot = jnp.where(hit1 | hit2, 1.0, 0.0)
    ji = lax.broadcasted_iota(jnp.int32, (rt, rt), 0)
    ti = lax.broadcasted_iota(jnp.int32, (rt, rt), 1)
    before = jnp.where(ji < ti, 1.0, 0.0).astype(BF16)
    base = cnt_ref[...][:, 0:1]
    prior = _dot(onehot.astype(BF16), before) + base
    r1 = jnp.sum(jnp.where(hit1, prior, 0.0), axis=0, keepdims=True)
    r2 = jnp.sum(jnp.where(hit2, prior, 0.0), axis=0, keepdims=True)
    cnt_ref[...] = cnt_ref[...] + jnp.sum(onehot, axis=1, keepdims=True)

    zi = jnp.zeros((4, rt), jnp.int32)
    oi_ref[...] = jnp.concatenate([i1, i2, r1.astype(jnp.int32), r2.astype(jnp.int32), zi], axis=0)
    zf = jnp.zeros((6, rt), F32)
    of_ref[...] = jnp.concatenate([g1 / gsum, g2 / gsum, zf], axis=0)


def _routing(logits_t, router_bias):
    ne, t = logits_t.shape
    rt = min(512, t)
    bias = jnp.broadcast_to(router_bias.astype(F32)[:, None], (ne, LANE))
    return pl.pallas_call(
        _route_kernel,
        grid=(t // rt,),
        in_specs=[pl.BlockSpec((ne, rt), lambda i: (0, i)), pl.BlockSpec((ne, LANE), lambda i: (0, 0))],
        out_specs=[
            pl.BlockSpec((8, rt), lambda i: (0, i)),
            pl.BlockSpec((8, rt), lambda i: (0, i)),
            pl.BlockSpec((ne, LANE), lambda i: (0, 0)),
        ],
        out_shape=[
            jax.ShapeDtypeStruct((8, t), jnp.int32),
            jax.ShapeDtypeStruct((8, t), F32),
            jax.ShapeDtypeStruct((ne, LANE), F32),
        ],
        compiler_params=_cparams(("arbitrary",)),
        name="moe_routing",
    )(logits_t, bias)


def _moe_kernel(blk_e, src_tok, dst_row, h_hbm, wg_ref, wu_ref, wd_ref, out_hbm, xbuf, ybuf, gsem, ssem):
    i = pl.program_id(0)
    nblk = pl.num_programs(0)
    rc = ROW_CHUNKS
    rb = xbuf.shape[1] // rc
    slot = lax.rem(i, MOE_SLOTS)
    slot_next2 = lax.rem(i + 2, MOE_SLOTS)

    def gather_start(plan_blk, s, j):
        tok = pl.multiple_of(src_tok[plan_blk * rb + j] * rc, rc)
        pltpu.make_async_copy(h_hbm.at[pl.ds(tok, rc), :], xbuf.at[s, pl.ds(j * rc, rc), :], gsem.at[s]).start()

    def scatter_start(plan_blk, s, j):
        row = pl.multiple_of(dst_row[plan_blk * rb + j] * rc, rc)
        pltpu.make_async_copy(ybuf.at[s, pl.ds(j * rc, rc), :], out_hbm.at[pl.ds(row, rc), :], ssem.at[s]).start()

    def gather_wait(s):
        pltpu.make_async_copy(h_hbm.at[pl.ds(0, rb * rc), :], xbuf.at[s], gsem.at[s]).wait()

    def scatter_wait(s):
        pltpu.make_async_copy(ybuf.at[s], out_hbm.at[pl.ds(0, rb * rc), :], ssem.at[s]).wait()

    @pl.when(i == 0)
    def _():
        ybuf[...] = jnp.zeros_like(ybuf)
        for j in range(rb):
            gather_start(1, 0, j)
            gather_start(2, 1, j)

    gather_wait(slot)

    @pl.when(i >= 2)
    def _():
        scatter_wait(slot)

    x = jnp.concatenate([xbuf[slot, pl.ds(c, rb, stride=rc), :] for c in range(rc)], axis=1).astype(BF16)
    for j in range(rb):
        gather_start(i + 3, slot_next2, j)
        scatter_start(i, slot_next2, j)
    hid = _silu(_dot(x, wg_ref[0])) * _dot(x, wu_ref[0])
    y = _dot(hid.astype(BF16), wd_ref[0])
    for c in range(rc):
        ybuf[slot, pl.ds(c, rb, stride=rc), :] = y[:, c * LANE:(c + 1) * LANE]

    @pl.when(i == nblk - 1)
    def _():
        for j in range(rb):
            scatter_start(i + 1, slot, j)
        for s in range(MOE_SLOTS):
            scatter_wait(s)
        gather_wait(lax.rem(i + 1, MOE_SLOTS))
        gather_wait(slot_next2)


def _experts(blk_e, src_tok, dst_row, h2, wg, wu, wd, layer, rb):
    rc = ROW_CHUNKS
    _, ne, d, dff = wg.shape
    nblk = blk_e.shape[0]
    out_rows = (nblk + 1) * rb
    return pl.pallas_call(
        _moe_kernel,
        grid_spec=pltpu.PrefetchScalarGridSpec(
            num_scalar_prefetch=3,
            grid=(nblk,),
            in_specs=[
                pl.BlockSpec(memory_space=pl.ANY),
                pl.BlockSpec((None, 1, d, dff), lambda i, be, st, dr: (layer, be[i], 0, 0)),
                pl.BlockSpec((None, 1, d, dff), lambda i, be, st, dr: (layer, be[i], 0, 0)),
                pl.BlockSpec((None, 1, dff, d), lambda i, be, st, dr: (layer, be[i], 0, 0)),
            ],
            out_specs=pl.BlockSpec(memory_space=pl.ANY),
            scratch_shapes=[
                pltpu.VMEM((MOE_SLOTS, rb * rc, LANE), F32),
                pltpu.VMEM((MOE_SLOTS, rb * rc, LANE), F32),
                pltpu.SemaphoreType.DMA((MOE_SLOTS,)),
                pltpu.SemaphoreType.DMA((MOE_SLOTS,)),
            ],
        ),
        out_shape=jax.ShapeDtypeStruct((out_rows * rc, LANE), F32),
        compiler_params=_cparams(("arbitrary",)),
        name="moe_experts",
    )(blk_e, src_tok, dst_row, h2, wg, wu, wd)


def _combine_kernel(x_ref, y0_ref, y1_ref, w_ref, gate_ref, g_ref, o_ref, *, final):
    w = w_ref[...]
    tm = x_ref.shape[0]
    rows = lambda ref: jnp.concatenate([ref[pl.ds(c, tm, stride=ROW_CHUNKS), :] for c in range(ROW_CHUNKS)], axis=1)
    moe = w[:, 0:1] * rows(y0_ref) + w[:, 1:2] * rows(y1_ref)
    x = x_ref[...] + gate_ref[0] * moe
    if final:
        x = _rms(x, g_ref[...])
    o_ref[...] = x


def _combine(x2, y2, gates_t, mod6, final_g, seq, final):
    t, d = x2.shape
    tm = min(512, seq)
    per_b = seq // tm
    nt = t // tm
    return pl.pallas_call(
        functools.partial(_combine_kernel, final=final),
        grid=(nt,),
        in_specs=[
            pl.BlockSpec((tm, d), lambda i: (i, 0)),
            pl.BlockSpec((tm * ROW_CHUNKS, LANE), lambda i: (i, 0)),
            pl.BlockSpec((tm * ROW_CHUNKS, LANE), lambda i: (nt + i, 0)),
            pl.BlockSpec((tm, 2), lambda i: (i, 0)),
            pl.BlockSpec((1, 1, d), lambda i: ((i // per_b) * N_MOD + 5, 0, 0)),
            pl.BlockSpec((1, d), lambda i: (0, 0)),
        ],
        out_specs=pl.BlockSpec((tm, d), lambda i: (i, 0)),
        out_shape=jax.ShapeDtypeStruct((t, d), F32),
        compiler_params=_cparams(("arbitrary",)),
        name="moe_combine",
    )(x2, y2, y2, gates_t, mod6, final_g.reshape(1, d))


def _permute_kernel(w_ref, o_ref):
    gw = GROUP_W
    kv = SWA_KV_HEADS * HEAD_DIM
    xbc = gw + 2 * SSM_GROUPS * SSM_STATE
    groups = [(3 * gw, U_A), (gw, U_BQ), (kv, U_BK), (kv, U_BV), (gw, U_Z), (xbc, U_XBC), (N_HEADS, U_DT),
              (3 * gw, U_DQ)]
    o_ref[0, :, U_DT * LANE:N_UNITS * LANE] = jnp.zeros((o_ref.shape[1], (N_UNITS - U_DT) * LANE), o_ref.dtype)
    src = 0
    for width, unit in groups:
        o_ref[0, :, unit * LANE:unit * LANE + width] = w_ref[0, :, src:src + width].astype(o_ref.dtype)
        src += width


def _permute_w_in(w_in):
    depth, d, n_in = w_in.shape
    tr = 256
    return pl.pallas_call(
        _permute_kernel,
        grid=(depth, d // tr),
        in_specs=[pl.BlockSpec((1, tr, n_in), lambda l, i: (l, i, 0))],
        out_specs=pl.BlockSpec((1, tr, NP), lambda l, i: (l, i, 0)),
        out_shape=jax.ShapeDtypeStruct((depth, d, NP), BF16),
        compiler_params=_cparams(("arbitrary", "arbitrary")),
        name="permute_w_in",
    )(w_in)


def _place_kernel(dest_ref, o_ref):
    def init(p, carry):
        o_ref[p] = jnp.int32(-1)
        return carry

    def put(j, carry):
        o_ref[dest_ref[j]] = j
        return carry

    lax.fori_loop(0, o_ref.shape[0], init, 0, unroll=8)
    lax.fori_loop(0, dest_ref.shape[0], put, 0, unroll=8)


def _place_rows(dest, npos):
    return pl.pallas_call(
        _place_kernel,
        in_specs=[pl.BlockSpec(memory_space=pltpu.SMEM)],
        out_specs=pl.BlockSpec(memory_space=pltpu.SMEM),
        out_shape=jax.ShapeDtypeStruct((npos,), jnp.int32),
        name="moe_place_rows",
    )(dest)


def _dispatch_plan(idx, cnt, t, rb):
    ne = N_EXPERTS
    counts = cnt[:, 0].astype(jnp.int32)
    padded = (counts + rb - 1) // rb * rb
    pad_ends = jnp.cumsum(padded)
    pad_starts = pad_ends - padded
    e_idx = idx[0:2]
    rank = idx[2:4]
    onehot = e_idx[:, :, None] == jnp.arange(ne, dtype=jnp.int32)
    dest = jnp.sum(jnp.where(onehot, pad_starts, 0), axis=-1) + rank + rb
    nblk = (2 * t + ne * rb) // rb
    npos = (nblk + 3) * rb
    placed = _place_rows(dest.reshape(-1), npos)
    is_pad = placed < 0
    dump_row = 2 * t + jnp.cumsum(is_pad.astype(jnp.int32)) - 1
    dst_row = jnp.where(is_pad, dump_row, placed)
    src_tok = jnp.where(is_pad, 0, jnp.where(placed >= t, placed - t, placed))
    blk_start = jnp.arange(nblk, dtype=jnp.int32) * rb
    blk_e = jnp.minimum(jnp.sum(pad_ends[None, :] <= blk_start[:, None], axis=1), ne - 1).astype(jnp.int32)
    return blk_e, src_tok, dst_row


def kernel(x, c, w_mod, b_mod, norm1_g, norm2_g, w_in, w_out, conv_a_w, gn_a, attn_sinks, gn_b, ssm_conv_w,
           ssm_conv_b, dt_bias, a_log, d_skip, ssm_norm_g, gn_d, router_w, router_bias, moe_w_gate, moe_w_up,
           moe_w_down, final_g):
    batch, seq, d = x.shape
    assert d == ROW_CHUNKS * LANE
    depth = w_in.shape[0]
    t = batch * seq
    rb = 256

    mod = _modulation(c, w_mod, b_mod)
    w_in_p = _permute_w_in(w_in)
    w_out_b = w_out.astype(BF16)
    wg_b, wu_b, wd_b = moe_w_gate.astype(BF16), moe_w_up.astype(BF16), moe_w_down.astype(BF16)
    router_w_p = jnp.zeros((d, LANE), F32).at[:, :N_EXPERTS].set(router_w)

    x2 = x.reshape(t, d)
    for l in range(depth):
        mod6 = mod[l].reshape(batch * N_MOD, 1, d)
        proj = _in_projection(x2, norm1_g[l], mod6, w_in_p, l, seq)
        ya = _mixer_a(proj, conv_a_w[l], gn_a[l], batch, seq)
        yb = _mixer_b(proj, attn_sinks[l], gn_b[l], batch, seq)
        yc = _mixer_c(proj, ssm_conv_w[l], ssm_conv_b[l], dt_bias[l], a_log[l], d_skip[l], ssm_norm_g[l],
                      batch, seq)
        yd = _mixer_d(proj, batch, seq)
        x2, h2, logits_t = _out_projection(ya, yb, yc, yd, gn_d[l], w_out_b, l, x2, mod6, norm2_g[l],
                                           router_w_p, seq)
        idx, gates, cnt = _routing(logits_t, router_bias)
        blk_e, src_tok, dst_row = _dispatch_plan(idx, cnt, t, rb)
        y2 = _experts(blk_e, src_tok, dst_row, h2, wg_b, wu_b, wd_b, l, rb)
        x2 = _combine(x2, y2, gates[0:2].T, mod6, final_g, seq, final=(l == depth - 1))
    return x2.reshape(batch, seq, d)
```

```python
import functools
import math

import jax
import jax.numpy as jnp
from jax import lax
from jax.experimental import pallas as pl
from jax.experimental.pallas import tpu as pltpu

F32 = jnp.float32
BF16 = jnp.bfloat16

EPS = 1e-6
HEAD_DIM = 64
GROUP_W = 512
N_HEADS = GROUP_W // HEAD_DIM
SWA_KV_HEADS = 2
SWA_WINDOW = 128
SSM_GROUPS = 2
SSM_STATE = 128
SSM_CHUNK = 128
N_EXPERTS = 16
N_EXPERT_GROUPS = 4
EXPERTS_PER_GROUP = N_EXPERTS // N_EXPERT_GROUPS
N_MOD = 6

LANE = 128
HALO = 16
SB_KEYS = 256
ROW_CHUNKS = 16
MOE_SLOTS = 3
VMEM_LIMIT = 56 * 1024 * 1024

U_A = 0
U_DQ = 12
U_DK = 16
U_DV = 20
U_Z = 24
U_BQ = 28
U_XBC = 32
U_BK = 40
U_BV = 41
U_DT = 42
N_UNITS = 44
NP = N_UNITS * LANE

PROJ_DTYPE = jnp.bfloat16


def _cparams(sem, vmem=VMEM_LIMIT):
    return pltpu.CompilerParams(dimension_semantics=sem, vmem_limit_bytes=vmem)


def _rms(y, g):
    return y * lax.rsqrt(jnp.mean(y * y, axis=-1, keepdims=True) + EPS) * g


def _split_bf16(a):
    hi = a.astype(BF16)
    lo = (a - hi.astype(F32)).astype(BF16)
    return hi, lo


def _dot(a, b):
    return jnp.dot(a, b, preferred_element_type=F32)


def _dot3(a, b):
    ah, al = _split_bf16(a)
    bh, bl = _split_bf16(b)
    return _dot(ah, bh) + _dot(al, bh) + _dot(ah, bl)


def _silu(a):
    return a * (1.0 / (1.0 + jnp.exp(-a)))


def _softplus(a):
    return jnp.maximum(a, 0.0) + jnp.log1p(jnp.exp(-jnp.abs(a)))


def _mod_kernel(c_ref, w_ref, b_ref, o_ref):
    cond = _silu(c_ref[...])
    o_ref[0] = _dot3(cond, w_ref[0]) + b_ref[0]


def _modulation(c, w_mod, b_mod):
    depth, d, nm = w_mod.shape
    b = c.shape[0]
    rows = 8
    cp = jnp.zeros((rows, d), F32).at[:b].set(c)
    tn = 1024
    out = pl.pallas_call(
        _mod_kernel,
        grid=(depth, nm // tn),
        in_specs=[
            pl.BlockSpec((rows, d), lambda l, j: (0, 0)),
            pl.BlockSpec((1, d, tn), lambda l, j: (l, 0, j)),
            pl.BlockSpec((1, 1, tn), lambda l, j: (l, 0, j)),
        ],
        out_specs=pl.BlockSpec((1, rows, tn), lambda l, j: (l, 0, j)),
        out_shape=jax.ShapeDtypeStruct((depth, rows, nm), F32),
        compiler_params=_cparams(("arbitrary", "arbitrary")),
        name="modulation",
    )(cp, w_mod, b_mod.reshape(depth, 1, nm))
    return out[:, :b]


def _inproj_kernel(x_ref, g_ref, sh_ref, sc_ref, w_ref, o_ref, h_scr):
    @pl.when(pl.program_id(1) == 0)
    def _():
        chunk = LANE

        def body(r, carry):
            rows = pl.ds(pl.multiple_of(r * chunk, chunk), chunk)
            h = _rms(x_ref[rows, :], g_ref[...]) * (1.0 + sc_ref[0]) + sh_ref[0]
            h_scr[rows, :] = h.astype(BF16)
            return carry

        lax.fori_loop(0, x_ref.shape[0] // chunk, body, 0)

    o_ref[...] = _dot(h_scr[...], w_ref[...]).astype(o_ref.dtype)


def _in_projection(x2, g, mod6, w_in_p, layer, seq):
    t, d = x2.shape
    tm = min(1024, seq)
    tn = 512
    per_b = seq // tm
    return pl.pallas_call(
        _inproj_kernel,
        grid=(t // tm, NP // tn),
        in_specs=[
            pl.BlockSpec((tm, d), lambda i, j: (i, 0)),
            pl.BlockSpec((1, d), lambda i, j: (0, 0)),
            pl.BlockSpec((1, 1, d), lambda i, j: ((i // per_b) * N_MOD + 0, 0, 0)),
            pl.BlockSpec((1, 1, d), lambda i, j: ((i // per_b) * N_MOD + 1, 0, 0)),
            pl.BlockSpec((None, d, tn), lambda i, j: (layer, 0, j)),
        ],
        out_specs=pl.BlockSpec((tm, tn), lambda i, j: (i, j)),
        out_shape=jax.ShapeDtypeStruct((t, NP), PROJ_DTYPE),
        scratch_shapes=[pltpu.VMEM((tm, d), BF16)],
        compiler_params=_cparams(("arbitrary", "arbitrary")),
        name="adaln_inproj",
    )(x2, g.reshape(1, d), mod6, mod6, w_in_p)


def _mixa_kernel(p_ref, halo_ref, w_ref, g_ref, o_ref, ext_scr):
    ts = p_ref.shape[0]
    w = GROUP_W
    p = p_ref[...].astype(F32)
    hp = halo_ref[...].astype(F32)
    u = p[:, w:2 * w] * p[:, 2 * w:3 * w]
    hu = hp[:, w:2 * w] * hp[:, 2 * w:3 * w]
    hu = jnp.where(pl.program_id(1) == 0, 0.0, hu)
    ext_scr[0:HALO, :] = hu
    ext_scr[HALO:HALO + ts, :] = u
    cw = w_ref[...]
    conv = (cw[0:1] * ext_scr[HALO - 2:HALO - 2 + ts, :]
            + cw[1:2] * ext_scr[HALO - 1:HALO - 1 + ts, :]
            + cw[2:3] * u)
    o_ref[...] = _rms(p[:, 0:w] * conv, g_ref[...]).astype(o_ref.dtype)


def _mixer_a(proj, conv_w, gn, batch, seq):
    t = proj.shape[0]
    ts = min(512, seq)
    nt = seq // ts
    hb = ts // HALO
    wa = 3 * GROUP_W
    return pl.pallas_call(
        _mixa_kernel,
        grid=(batch, nt),
        in_specs=[
            pl.BlockSpec((ts, wa), lambda b, i: (b * nt + i, U_A * LANE // wa)),
            pl.BlockSpec((HALO, wa), lambda b, i: (jnp.maximum((b * nt + i) * hb - 1, 0), U_A * LANE // wa)),
            pl.BlockSpec(conv_w.shape, lambda b, i: (0, 0)),
            pl.BlockSpec((1, GROUP_W), lambda b, i: (0, 0)),
        ],
        out_specs=pl.BlockSpec((ts, GROUP_W), lambda b, i: (b * nt + i, 0)),
        out_shape=jax.ShapeDtypeStruct((t, GROUP_W), BF16),
        scratch_shapes=[pltpu.VMEM((HALO + ts, GROUP_W), F32)],
        compiler_params=_cparams(("arbitrary", "arbitrary")),
        name="mixer_a_conv",
    )(proj, proj, conv_w, gn.reshape(1, GROUP_W))


def _swa_kernel(q_ref, k_ref, kp_ref, v_ref, vp_ref, sink_ref, slope_ref, g_ref, o_ref, y_scr):
    w = SWA_WINDOW
    hd = HEAD_DIM
    grp = N_HEADS // SWA_KV_HEADS
    nblk = q_ref.shape[0] // w
    first = pl.program_id(1) == 0
    q = (q_ref[...].astype(F32) * (1.0 / math.sqrt(hd))).astype(BF16)
    k = jnp.concatenate([kp_ref[...], k_ref[...]], axis=0).astype(BF16)
    v = jnp.concatenate([vp_ref[...], v_ref[...]], axis=0).astype(BF16)
    rows = grp * w
    qi = lax.broadcasted_iota(jnp.int32, (rows, 2 * w), 0) % w
    kj = lax.broadcasted_iota(jnp.int32, (rows, 2 * w), 1)
    hrow = lax.broadcasted_iota(jnp.int32, (rows, 1), 0) // w
    dist = qi + w - kj
    valid = (dist >= 0) & (dist < w)
    distf = dist.astype(F32)
    sinks = sink_ref[...]
    slopes = slope_ref[...]

    chains = []
    for kvh in range(SWA_KV_HEADS):
        slope = jnp.zeros((rows, 1), F32)
        sink = jnp.zeros((rows, 1), F32)
        for gi in range(grp):
            h = kvh * grp + gi
            slope = jnp.where(hrow == gi, slopes[0:1, h:h + 1], slope)
            sink = jnp.where(hrow == gi, sinks[0:1, h:h + 1], sink)
        bias = jnp.where(valid, -slope * distf, -jnp.inf)
        bias0 = jnp.where(first & (kj < w), -jnp.inf, bias)
        for n in range(nblk):
            qs = jnp.concatenate([q[n * w:(n + 1) * w, (kvh * grp + gi) * hd:(kvh * grp + gi + 1) * hd]
                                  for gi in range(grp)], axis=0)
            kn = k[n * w:(n + 2) * w, kvh * hd:(kvh + 1) * hd]
            vn = v[n * w:(n + 2) * w, kvh * hd:(kvh + 1) * hd]
            chains.append((kvh, n, qs, kn, vn, bias0 if n == 0 else bias, sink))

    scores = [lax.dot_general(qs, kn, (((1,), (1,)), ((), ())), preferred_element_type=F32) + bias
              for _, _, qs, kn, _, bias, _ in chains]
    es, rden = [], []
    for (_, _, _, _, _, _, sink), s in zip(chains, scores):
        m = jnp.maximum(jnp.max(s, axis=-1, keepdims=True), sink)
        e = jnp.exp(s - m)
        rden.append(1.0 / (jnp.sum(e, axis=-1, keepdims=True) + jnp.exp(sink - m)))
        es.append(e.astype(BF16))
    outs = [_dot(e, vn) * r for (_, _, _, _, vn, _, _), e, r in zip(chains, es, rden)]
    for (kvh, n, _, _, _, _, _), o in zip(chains, outs):
        for gi in range(grp):
            h = kvh * grp + gi
            y_scr[n * w:(n + 1) * w, h * hd:(h + 1) * hd] = o[gi * w:(gi + 1) * w, :]
    o_ref[...] = _rms(y_scr[...], g_ref[...]).astype(o_ref.dtype)


def _mixer_b(proj, sinks, gn, batch, seq):
    t = proj.shape[0]
    w = SWA_WINDOW
    ts = min(512, seq)
    nt = seq // ts
    pb = ts // w
    slopes = 2.0 ** (-8.0 * jnp.arange(1, N_HEADS + 1, dtype=F32) / N_HEADS)
    pad = lambda a: jnp.zeros((1, LANE), F32).at[0, :N_HEADS].set(a)
    cur = lambda u: (lambda b, i: (b * nt + i, u))
    prev = lambda u: (lambda b, i: (jnp.maximum((b * nt + i) * pb - 1, 0), u))
    return pl.pallas_call(
        _swa_kernel,
        grid=(batch, nt),
        in_specs=[
            pl.BlockSpec((ts, GROUP_W), cur(U_BQ * LANE // GROUP_W)),
            pl.BlockSpec((ts, LANE), cur(U_BK)),
            pl.BlockSpec((w, LANE), prev(U_BK)),
            pl.BlockSpec((ts, LANE), cur(U_BV)),
            pl.BlockSpec((w, LANE), prev(U_BV)),
            pl.BlockSpec((1, LANE), lambda b, i: (0, 0)),
            pl.BlockSpec((1, LANE), lambda b, i: (0, 0)),
            pl.BlockSpec((1, GROUP_W), lambda b, i: (0, 0)),
        ],
        out_specs=pl.BlockSpec((ts, GROUP_W), lambda b, i: (b * nt + i, 0)),
        out_shape=jax.ShapeDtypeStruct((t, GROUP_W), BF16),
        scratch_shapes=[pltpu.VMEM((ts, GROUP_W), F32)],
        compiler_params=_cparams(("arbitrary", "arbitrary")),
        name="mixer_b_swa",
    )(proj, proj, proj, proj, proj, pad(sinks), pad(slopes), gn.reshape(1, GROUP_W))


def _ssd_kernel(z_ref, xbc_ref, halo_ref, dt_ref, cw_ref, cb_ref, dtb_ref, alog_ref, dsk_ref, g_ref,
                o_ref, ext_scr, y_scr, off_scr, state_scr):
    l = SSM_CHUNK
    hd = HEAD_DIM
    n = SSM_STATE
    rep = N_HEADS // SSM_GROUPS
    first = pl.program_id(1) == 0

    @pl.when(first)
    def _():
        state_scr[...] = jnp.zeros_like(state_scr)

    raw = xbc_ref[...].astype(F32)
    halo = jnp.where(first, 0.0, halo_ref[...].astype(F32))
    ext_scr[0:HALO, :] = halo
    ext_scr[HALO:HALO + l, :] = raw
    cw = cw_ref[...]
    conv = (cw[0:1] * ext_scr[HALO - 3:HALO - 3 + l, :]
            + cw[1:2] * ext_scr[HALO - 2:HALO - 2 + l, :]
            + cw[2:3] * ext_scr[HALO - 1:HALO - 1 + l, :]
            + cw[3:4] * raw)
    xbc = _silu(conv + cb_ref[...])
    xs = xbc[:, 0:GROUP_W]
    bm = xbc[:, GROUP_W:GROUP_W + SSM_GROUPS * n]
    cm = xbc[:, GROUP_W + SSM_GROUPS * n:GROUP_W + 2 * SSM_GROUPS * n]

    dt = _softplus(dt_ref[...].astype(F32) + dtb_ref[...])
    da = dt * (-jnp.exp(alog_ref[...]))
    ri = lax.broadcasted_iota(jnp.int32, (l, l), 0)
    ci = lax.broadcasted_iota(jnp.int32, (l, l), 1)
    causal = ri >= ci
    tril = jnp.where(causal, 1.0, 0.0).astype(BF16)
    da_hi, da_lo = _split_bf16(da)
    cum = _dot(tril, da_hi) + _dot(tril, da_lo)
    cum_t = cum.T
    last = cum[l - 1:l, :]
    er = lax.broadcasted_iota(jnp.int32, (LANE, GROUP_W), 0)
    ec = lax.broadcasted_iota(jnp.int32, (LANE, GROUP_W), 1)
    spread = jnp.where(ec // hd == er, 1.0, 0.0).astype(BF16)

    def per_head_lanes(a):
        a_hi, a_lo = _split_bf16(a)
        return _dot(a_hi, spread) + _dot(a_lo, spread)

    xdt_all = xs * per_head_lanes(dt)
    xend_all = xdt_all * per_head_lanes(jnp.exp(last - cum))
    keep_prev = jnp.exp(last)

    for gi in range(SSM_GROUPS):
        bg = bm[:, gi * n:(gi + 1) * n]
        cg = cm[:, gi * n:(gi + 1) * n].astype(BF16)
        bg_t = bg.T.astype(BF16)
        cb = _dot(cg, bg_t)
        for hi_ in range(rep):
            h = gi * rep + hi_
            lanes = slice(h * hd, (h + 1) * hd)
            col = cum[:, h:h + 1]
            row = cum_t[h:h + 1, :]
            decay = jnp.exp(jnp.where(causal, col - row, -jnp.inf))
            y_scr[:, lanes] = _dot((cb * decay).astype(BF16), xdt_all[:, lanes].astype(BF16))
            prev = state_scr[h]
            off_scr[:, lanes] = _dot(cg, prev.astype(BF16))
            new = _dot(bg_t, xend_all[:, lanes].astype(BF16))
            state_scr[h] = prev * keep_prev[0:1, h:h + 1] + new

    y = y_scr[...] + off_scr[...] * per_head_lanes(jnp.exp(cum)) + xs * dsk_ref[...]
    y = y * _silu(z_ref[...].astype(F32))
    o_ref[...] = _rms(y, g_ref[...]).astype(o_ref.dtype)


def _mixer_c(proj, conv_w, conv_b, dt_bias, a_log, d_skip, norm_g, batch, seq):
    t = proj.shape[0]
    l = SSM_CHUNK
    nc = seq // l
    wx = GROUP_W + 2 * SSM_GROUPS * SSM_STATE
    hb = l // HALO
    pad = lambda a: jnp.zeros((1, LANE), F32).at[0, :N_HEADS].set(a)
    cur = lambda u: (lambda b, c: (b * nc + c, u))
    return pl.pallas_call(
        _ssd_kernel,
        grid=(batch, nc),
        in_specs=[
            pl.BlockSpec((l, GROUP_W), cur(U_Z * LANE // GROUP_W)),
            pl.BlockSpec((l, wx), cur(U_XBC * LANE // wx)),
            pl.BlockSpec((HALO, wx), lambda b, c: (jnp.maximum((b * nc + c) * hb - 1, 0), U_XBC * LANE // wx)),
            pl.BlockSpec((l, LANE), cur(U_DT)),
            pl.BlockSpec(conv_w.shape, lambda b, c: (0, 0)),
            pl.BlockSpec((1, wx), lambda b, c: (0, 0)),
            pl.BlockSpec((1, LANE), lambda b, c: (0, 0)),
            pl.BlockSpec((1, LANE), lambda b, c: (0, 0)),
            pl.BlockSpec((1, GROUP_W), lambda b, c: (0, 0)),
            pl.BlockSpec((1, GROUP_W), lambda b, c: (0, 0)),
        ],
        out_specs=pl.BlockSpec((l, GROUP_W), lambda b, c: (b * nc + c, 0)),
        out_shape=jax.ShapeDtypeStruct((t, GROUP_W), BF16),
        scratch_shapes=[
            pltpu.VMEM((HALO + l, wx), F32),
            pltpu.VMEM((l, GROUP_W), F32),
            pltpu.VMEM((l, GROUP_W), F32),
            pltpu.VMEM((N_HEADS, SSM_STATE, HEAD_DIM), F32),
        ],
        compiler_params=_cparams(("arbitrary", "arbitrary")),
        name="mixer_c_ssd",
    )(proj, proj, proj, proj, conv_w, conv_b.reshape(1, wx), pad(dt_bias), pad(a_log),
      jnp.repeat(d_skip.astype(F32), HEAD_DIM).reshape(1, GROUP_W), norm_g.reshape(1, GROUP_W))


def _sb_kernel(q_ref, k_ref, v_ref, o_ref):
    kb_rows = SB_KEYS
    tq = q_ref.shape[0]
    nr = tq // kb_rows
    hd = HEAD_DIM
    nh = LANE // hd
    qi = pl.program_id(2)
    ri = lax.broadcasted_iota(jnp.int32, (kb_rows, kb_rows), 0)
    ci = lax.broadcasted_iota(jnp.int32, (kb_rows, kb_rows), 1)
    strict = ci < ri
    later = jnp.where(ri > ci, 1.0, 0.0).astype(BF16)
    lanes = [slice(h * hd, (h + 1) * hd) for h in range(nh)]
    qscale = math.log2(math.e) / math.sqrt(hd)
    qs = [[(q_ref[r * kb_rows:(r + 1) * kb_rows, ln].astype(F32) * qscale).astype(BF16) for r in range(nr)]
          for ln in lanes]
    sign = jnp.uint32(0x80000000)

    def stages(items):
        zs = [lax.dot_general(q, k, (((1,), (1,)), ((), ())), preferred_element_type=F32)
              for q, k, _, _ in items]
        sps = []
        for (_, _, _, keep), z in zip(items, zs):
            neg_abs = lax.bitcast_convert_type(lax.bitcast_convert_type(z, jnp.uint32) | sign, F32)
            sp = jnp.maximum(z, 0.0) + jnp.log(1.0 + jnp.exp2(neg_abs)) * math.log2(math.e)
            sps.append(sp if keep is None else jnp.where(keep, sp, 0.0))
        afters = [_dot(sp.astype(BF16), later) for sp in sps]
        probs = []
        for (_, _, _, keep), z, sp, after in zip(items, zs, sps, afters):
            a = jnp.exp2(z - sp - after)
            probs.append((a if keep is None else jnp.where(keep, a, 0.0)).astype(BF16))
        outs = [_dot(a, v) for (_, _, v, _), a in zip(items, probs)]
        return [(jnp.sum(sp, axis=-1, keepdims=True), o) for sp, o in zip(sps, outs)]

    def step(plan, carry):
        kv = {}
        items = []
        for kb, c, kind in plan:
            h = c // nr
            if (id(kb), h) not in kv:
                rows = pl.ds(pl.multiple_of(kb * kb_rows, kb_rows), kb_rows)
                kv[(id(kb), h)] = (k_ref[rows, lanes[h]].astype(BF16), v_ref[rows, lanes[h]].astype(BF16))
            k, v = kv[(id(kb), h)]
            items.append((qs[h][c % nr], k, v, strict if kind == "diag" else None))
        carry = list(carry)
        for (_, c, _), (rsum, o) in zip(plan, stages(items)):
            run, acc = carry[c]
            carry[c] = (run + rsum, acc + jnp.exp2(-run) * o)
        return tuple(carry)

    nch = nh * nr
    carry = tuple((jnp.zeros((kb_rows, 1), F32), jnp.zeros((kb_rows, hd), F32)) for _ in range(nch))
    first = [qi * nr + d for d in range(nr)]
    plan = [(first[d], h * nr + r, "diag" if r == d else "full")
            for d in reversed(range(nr)) for h in range(nh) for r in range(d, nr)]
    carry = step(plan, carry)
    nfull = qi * nr
    per_it = 2 if nr % 2 == 0 else 1

    def body(j, c):
        kbs = [nfull - 1 - per_it * j - u for u in range(per_it)]
        return step([(kb, ch, "full") for kb in kbs for ch in range(nch)], c)

    carry = lax.fori_loop(0, nfull // per_it, body, carry)
    for h in range(nh):
        for r in range(nr):
            o_ref[r * kb_rows:(r + 1) * kb_rows, lanes[h]] = carry[h * nr + r][1].astype(o_ref.dtype)


def _mixer_d(proj, batch, seq):
    t = proj.shape[0]
    tq = min(2 * SB_KEYS, seq)
    nq = seq // tq
    pairs = GROUP_W // LANE
    return pl.pallas_call(
        _sb_kernel,
        grid=(batch, pairs, nq),
        in_specs=[
            pl.BlockSpec((tq, LANE), lambda b, p, i: (b * nq + i, U_DQ + p)),
            pl.BlockSpec((seq, LANE), lambda b, p, i: (b, U_DK + p)),
            pl.BlockSpec((seq, LANE), lambda b, p, i: (b, U_DV + p)),
        ],
        out_specs=pl.BlockSpec((tq, LANE), lambda b, p, i: (b * nq + i, p)),
        out_shape=jax.ShapeDtypeStruct((t, GROUP_W), BF16),
        compiler_params=_cparams(("arbitrary", "arbitrary", "arbitrary")),
        name="mixer_d_stickbreaking",
    )(proj, proj, proj)


def _outproj_kernel(ya_ref, yb_ref, yc_ref, yd_ref, gnd_ref, w_ref, x_ref, gate_ref, sh_ref, sc_ref, g_ref,
                    rw_ref, xo_ref, h_ref, lg_ref):
    w = GROUP_W
    yd = _rms(yd_ref[...].astype(F32), gnd_ref[...]).astype(BF16)
    mix = (_dot(ya_ref[...], w_ref[0:w, :]) + _dot(yb_ref[...], w_ref[w:2 * w, :])
           + _dot(yc_ref[...], w_ref[2 * w:3 * w, :]) + _dot(yd, w_ref[3 * w:4 * w, :]))
    x = x_ref[...] + gate_ref[0] * mix
    xo_ref[...] = x
    h = _rms(x, g_ref[...]) * (1.0 + sc_ref[0]) + sh_ref[0]
    for c in range(ROW_CHUNKS):
        h_ref[pl.ds(c, h.shape[0], stride=ROW_CHUNKS), :] = h[:, c * LANE:(c + 1) * LANE]
    lg_ref[...] = _dot3(h, rw_ref[...]).T[0:N_EXPERTS, :]


def _out_projection(ya, yb, yc, yd, gn_d, w_out_b, layer, x2, mod6, g2, router_w_p, seq):
    t, d = x2.shape
    tm = min(256, seq)
    per_b = seq // tm
    row = lambda i: (i, 0)
    const = lambda i: (0, 0)
    modspec = lambda m: pl.BlockSpec((1, 1, d), lambda i: ((i // per_b) * N_MOD + m, 0, 0))
    return pl.pallas_call(
        _outproj_kernel,
        grid=(t // tm,),
        in_specs=[
            pl.BlockSpec((tm, GROUP_W), row), pl.BlockSpec((tm, GROUP_W), row),
            pl.BlockSpec((tm, GROUP_W), row), pl.BlockSpec((tm, GROUP_W), row),
            pl.BlockSpec((1, GROUP_W), const),
            pl.BlockSpec((None, d, d), lambda i: (layer, 0, 0)),
            pl.BlockSpec((tm, d), row),
            modspec(2), modspec(3), modspec(4),
            pl.BlockSpec((1, d), const),
            pl.BlockSpec((d, LANE), const),
        ],
        out_specs=[
            pl.BlockSpec((tm, d), row),
            pl.BlockSpec((tm * ROW_CHUNKS, LANE), row),
            pl.BlockSpec((N_EXPERTS, tm), lambda i: (0, i)),
        ],
        out_shape=[
            jax.ShapeDtypeStruct((t, d), F32),
            jax.ShapeDtypeStruct((t * ROW_CHUNKS, LANE), F32),
            jax.ShapeDtypeStruct((N_EXPERTS, t), F32),
        ],
        compiler_params=_cparams(("arbitrary",)),
        name="outproj_adaln_router",
    )(ya, yb, yc, yd, gn_d.reshape(1, GROUP_W), w_out_b, x2, mod6, mod6, mod6, g2.reshape(1, d), router_w_p)


def _route_kernel(lg_ref, bias_ref, oi_ref, of_ref, cnt_ref):
    ne = N_EXPERTS
    rt = lg_ref.shape[1]

    @pl.when(pl.program_id(0) == 0)
    def _():
        cnt_ref[...] = jnp.zeros_like(cnt_ref)

    lg = lg_ref[...]
    e = jnp.exp(lg - jnp.max(lg, axis=0, keepdims=True))
    probs = e / jnp.sum(e, axis=0, keepdims=True)
    sel = probs + bias_ref[...][:, 0:1]
    rowi = lax.broadcasted_iota(jnp.int32, (ne, rt), 0)
    neg = -jnp.inf

    gscore = []
    for g in range(N_EXPERT_GROUPS):
        r = [sel[g * EXPERTS_PER_GROUP + i:g * EXPERTS_PER_GROUP + i + 1, :] for i in range(EXPERTS_PER_GROUP)]
        best = None
        for i in range(EXPERTS_PER_GROUP):
            for j in range(i + 1, EXPERTS_PER_GROUP):
                pair = jnp.maximum(r[i], r[j]) + jnp.minimum(r[i], r[j])
                best = pair if best is None else jnp.maximum(best, pair)
        gscore.append(best)
    top_g = jnp.zeros((1, rt), jnp.int32)
    top_s = gscore[0]
    for g in range(1, N_EXPERT_GROUPS):
        better = gscore[g] > top_s
        top_g = jnp.where(better, g, top_g)
        top_s = jnp.where(better, gscore[g], top_s)

    in_grp = (rowi // EXPERTS_PER_GROUP) == top_g
    masked = jnp.where(in_grp, sel, neg)
    m1 = jnp.max(masked, axis=0, keepdims=True)
    i1 = jnp.min(jnp.where(masked == m1, rowi, ne), axis=0, keepdims=True)
    masked2 = jnp.where(rowi == i1, neg, masked)
    m2 = jnp.max(masked2, axis=0, keepdims=True)
    i2 = jnp.min(jnp.where(masked2 == m2, rowi, ne), axis=0, keepdims=True)
    hit1 = rowi == i1
    hit2 = rowi == i2
    g1 = jnp.sum(jnp.where(hit1, probs, 0.0), axis=0, keepdims=True)
    g2 = jnp.sum(jnp.where(hit2, probs, 0.0), axis=0, keepdims=True)
    gsum = g1 + g2

    onehot = jnp.where(hit1 | hit2, 1.0, 0.0)
    ji = lax.broadcasted_iota(jnp.int32, (rt, rt), 0)
    ti = lax.broadcasted_iota(jnp.int32, (rt, rt), 1)
    before = jnp.where(ji < ti, 1.0, 0.0).astype(BF16)
    base = cnt_ref[...][:, 0:1]
    prior = _dot(onehot.astype(BF16), before) + base
    r1 = jnp.sum(jnp.where(hit1, prior, 0.0), axis=0, keepdims=True)
    r2 = jnp.sum(jnp.where(hit2, prior, 0.0), axis=0, keepdims=True)
    cnt_ref[...] = cnt_ref[...] + jnp.sum(onehot, axis=1, keepdims=True)

    zi = jnp.zeros((4, rt), jnp.int32)
    oi_ref[...] = jnp.concatenate([i1, i2, r1.astype(jnp.int32), r2.astype(jnp.int32), zi], axis=0)
    zf = jnp.zeros((6, rt), F32)
    of_ref[...] = jnp.concatenate([g1 / gsum, g2 / gsum, zf], axis=0)


def _routing(logits_t, router_bias):
    ne, t = logits_t.shape
    rt = min(512, t)
    bias = jnp.broadcast_to(router_bias.astype(F32)[:, None], (ne, LANE))
    return pl.pallas_call(
        _route_kernel,
        grid=(t // rt,),
        in_specs=[pl.BlockSpec((ne, rt), lambda i: (0, i)), pl.BlockSpec((ne, LANE), lambda i: (0, 0))],
        out_specs=[
            pl.BlockSpec((8, rt), lambda i: (0, i)),
            pl.BlockSpec((8, rt), lambda i: (0, i)),
            pl.BlockSpec((ne, LANE), lambda i: (0, 0)),
        ],
        out_shape=[
            jax.ShapeDtypeStruct((8, t), jnp.int32),
            jax.ShapeDtypeStruct((8, t), F32),
            jax.ShapeDtypeStruct((ne, LANE), F32),
        ],
        compiler_params=_cparams(("arbitrary",)),
        name="moe_routing",
    )(logits_t, bias)


def _moe_kernel(blk_e, src_tok, dst_row, h_hbm, wg_ref, wu_ref, wd_ref, out_hbm, xbuf, ybuf, gsem, ssem):
    i = pl.program_id(0)
    nblk = pl.num_programs(0)
    rc = ROW_CHUNKS
    rb = xbuf.shape[1] // rc
    slot = lax.rem(i, MOE_SLOTS)
    slot_next2 = lax.rem(i + 2, MOE_SLOTS)

    def gather_start(plan_blk, s, j):
        tok = pl.multiple_of(src_tok[plan_blk * rb + j] * rc, rc)
        pltpu.make_async_copy(h_hbm.at[pl.ds(tok, rc), :], xbuf.at[s, pl.ds(j * rc, rc), :], gsem.at[s]).start()

    def scatter_start(plan_blk, s, j):
        row = pl.multiple_of(dst_row[plan_blk * rb + j] * rc, rc)
        pltpu.make_async_copy(ybuf.at[s, pl.ds(j * rc, rc), :], out_hbm.at[pl.ds(row, rc), :], ssem.at[s]).start()

    def gather_wait(s):
        pltpu.make_async_copy(h_hbm.at[pl.ds(0, rb * rc), :], xbuf.at[s], gsem.at[s]).wait()

    def scatter_wait(s):
        pltpu.make_async_copy(ybuf.at[s], out_hbm.at[pl.ds(0, rb * rc), :], ssem.at[s]).wait()

    @pl.when(i == 0)
    def _():
        ybuf[...] = jnp.zeros_like(ybuf)
        for j in range(rb):
            gather_start(1, 0, j)
            gather_start(2, 1, j)

    gather_wait(slot)

    @pl.when(i >= 2)
    def _():
        scatter_wait(slot)

    x = jnp.concatenate([xbuf[slot, pl.ds(c, rb, stride=rc), :] for c in range(rc)], axis=1).astype(BF16)
    for j in range(rb):
        gather_start(i + 3, slot_next2, j)
        scatter_start(i, slot_next2, j)
    hid = _silu(_dot(x, wg_ref[0])) * _dot(x, wu_ref[0])
    y = _dot(hid.astype(BF16), wd_ref[0])
    for c in range(rc):
        ybuf[slot, pl.ds(c, rb, stride=rc), :] = y[:, c * LANE:(c + 1) * LANE]

    @pl.when(i == nblk - 1)
    def _():
        for j in range(rb):
            scatter_start(i + 1, slot, j)
        for s in range(MOE_SLOTS):
            scatter_wait(s)
        gather_wait(lax.rem(i + 1, MOE_SLOTS))
        gather_wait(slot_next2)


def _experts(blk_e, src_tok, dst_row, h2, wg, wu, wd, layer, rb):
    rc = ROW_CHUNKS
    _, ne, d, dff = wg.shape
    nblk = blk_e.shape[0]
    out_rows = (nblk + 1) * rb
    return pl.pallas_call(
        _moe_kernel,
        grid_spec=pltpu.PrefetchScalarGridSpec(
            num_scalar_prefetch=3,
            grid=(nblk,),
            in_specs=[
                pl.BlockSpec(memory_space=pl.ANY),
                pl.BlockSpec((None, 1, d, dff), lambda i, be, st, dr: (layer, be[i], 0, 0)),
                pl.BlockSpec((None, 1, d, dff), lambda i, be, st, dr: (layer, be[i], 0, 0)),
                pl.BlockSpec((None, 1, dff, d), lambda i, be, st, dr: (layer, be[i], 0, 0)),
            ],
            out_specs=pl.BlockSpec(memory_space=pl.ANY),
            scratch_shapes=[
                pltpu.VMEM((MOE_SLOTS, rb * rc, LANE), F32),
                pltpu.VMEM((MOE_SLOTS, rb * rc, LANE), F32),
                pltpu.SemaphoreType.DMA((MOE_SLOTS,)),
                pltpu.SemaphoreType.DMA((MOE_SLOTS,)),
            ],
        ),
        out_shape=jax.ShapeDtypeStruct((out_rows * rc, LANE), F32),
        compiler_params=_cparams(("arbitrary",)),
        name="moe_experts",
    )(blk_e, src_tok, dst_row, h2, wg, wu, wd)


def _combine_kernel(x_ref, y0_ref, y1_ref, w_ref, gate_ref, g_ref, o_ref, *, final):
    w = w_ref[...]
    tm = x_ref.shape[0]
    w0 = w[:, 0:1]
    w1 = w[:, 1:2]
    gate = gate_ref[0]
    ssq = jnp.zeros((tm, 1), F32)
    for c in range(ROW_CHUNKS):
        cols = slice(c * LANE, (c + 1) * LANE)
        rows = pl.ds(c, tm, stride=ROW_CHUNKS)
        xc = x_ref[:, cols] + gate[:, cols] * (w0 * y0_ref[rows, :] + w1 * y1_ref[rows, :])
        o_ref[:, cols] = xc
        if final:
            ssq = ssq + jnp.sum(xc * xc, axis=-1, keepdims=True)
    if final:
        scale = lax.rsqrt(ssq * (1.0 / (ROW_CHUNKS * LANE)) + EPS)
        for c in range(ROW_CHUNKS):
            cols = slice(c * LANE, (c + 1) * LANE)
            o_ref[:, cols] = o_ref[:, cols] * scale * g_ref[:, cols]


def _combine(x2, y2, gates_t, mod6, final_g, seq, final):
    t, d = x2.shape
    tm = min(512, seq)
    per_b = seq // tm
    nt = t // tm
    return pl.pallas_call(
        functools.partial(_combine_kernel, final=final),
        grid=(nt,),
        in_specs=[
            pl.BlockSpec((tm, d), lambda i: (i, 0)),
            pl.BlockSpec((tm * ROW_CHUNKS, LANE), lambda i: (i, 0)),
            pl.BlockSpec((tm * ROW_CHUNKS, LANE), lambda i: (nt + i, 0)),
            pl.BlockSpec((tm, 2), lambda i: (i, 0)),
            pl.BlockSpec((1, 1, d), lambda i: ((i // per_b) * N_MOD + 5, 0, 0)),
            pl.BlockSpec((1, d), lambda i: (0, 0)),
        ],
        out_specs=pl.BlockSpec((tm, d), lambda i: (i, 0)),
        out_shape=jax.ShapeDtypeStruct((t, d), F32),
        compiler_params=_cparams(("arbitrary",)),
        name="moe_combine",
    )(x2, y2, y2, gates_t, mod6, final_g.reshape(1, d))


def _permute_kernel(w_ref, o_ref):
    gw = GROUP_W
    kv = SWA_KV_HEADS * HEAD_DIM
    xbc = gw + 2 * SSM_GROUPS * SSM_STATE
    groups = [(3 * gw, U_A), (gw, U_BQ), (kv, U_BK), (kv, U_BV), (gw, U_Z), (xbc, U_XBC), (N_HEADS, U_DT),
              (3 * gw, U_DQ)]
    o_ref[0, :, U_DT * LANE:N_UNITS * LANE] = jnp.zeros((o_ref.shape[1], (N_UNITS - U_DT) * LANE), o_ref.dtype)
    src = 0
    for width, unit in groups:
        o_ref[0, :, unit * LANE:unit * LANE + width] = w_ref[0, :, src:src + width].astype(o_ref.dtype)
        src += width


def _permute_w_in(w_in):
    depth, d, n_in = w_in.shape
    tr = 256
    return pl.pallas_call(
        _permute_kernel,
        grid=(depth, d // tr),
        in_specs=[pl.BlockSpec((1, tr, n_in), lambda l, i: (l, i, 0))],
        out_specs=pl.BlockSpec((1, tr, NP), lambda l, i: (l, i, 0)),
        out_shape=jax.ShapeDtypeStruct((depth, d, NP), BF16),
        compiler_params=_cparams(("arbitrary", "arbitrary")),
        name="permute_w_in",
    )(w_in)


def _place_kernel(dest_ref, o_ref):
    def init(p, carry):
        o_ref[p] = jnp.int32(-1)
        return carry

    def put(j, carry):
        o_ref[dest_ref[j]] = j
        return carry

    lax.fori_loop(0, o_ref.shape[0], init, 0, unroll=8)
    lax.fori_loop(0, dest_ref.shape[0], put, 0, unroll=8)


def _place_rows(dest, npos):
    return pl.pallas_call(
        _place_kernel,
        in_specs=[pl.BlockSpec(memory_space=pltpu.SMEM)],
        out_specs=pl.BlockSpec(memory_space=pltpu.SMEM),
        out_shape=jax.ShapeDtypeStruct((npos,), jnp.int32),
        name="moe_place_rows",
    )(dest)


def _dispatch_plan(idx, cnt, t, rb):
    ne = N_EXPERTS
    counts = cnt[:, 0].astype(jnp.int32)
    padded = (counts + rb - 1) // rb * rb
    pad_ends = jnp.cumsum(padded)
    pad_starts = pad_ends - padded
    e_idx = idx[0:2]
    rank = idx[2:4]
    onehot = e_idx[:, :, None] == jnp.arange(ne, dtype=jnp.int32)
    dest = jnp.sum(jnp.where(onehot, pad_starts, 0), axis=-1) + rank + rb
    nblk = (2 * t + ne * rb) // rb
    npos = (nblk + 3) * rb
    placed = _place_rows(dest.reshape(-1), npos)
    is_pad = placed < 0
    dump_row = 2 * t + jnp.cumsum(is_pad.astype(jnp.int32)) - 1
    dst_row = jnp.where(is_pad, dump_row, placed)
    src_tok = jnp.where(is_pad, 0, jnp.where(placed >= t, placed - t, placed))
    blk_start = jnp.arange(nblk, dtype=jnp.int32) * rb
    blk_e = jnp.minimum(jnp.sum(pad_ends[None, :] <= blk_start[:, None], axis=1), ne - 1).astype(jnp.int32)
    return blk_e, src_tok, dst_row


def kernel(x, c, w_mod, b_mod, norm1_g, norm2_g, w_in, w_out, conv_a_w, gn_a, attn_sinks, gn_b, ssm_conv_w,
           ssm_conv_b, dt_bias, a_log, d_skip, ssm_norm_g, gn_d, router_w, router_bias, moe_w_gate, moe_w_up,
           moe_w_down, final_g):
    batch, seq, d = x.shape
    assert d == ROW_CHUNKS * LANE
    depth = w_in.shape[0]
    t = batch * seq
    rb = 256

    mod = _modulation(c, w_mod, b_mod)
    w_in_p = _permute_w_in(w_in)
    w_out_b = w_out.astype(BF16)
    wg_b, wu_b, wd_b = moe_w_gate.astype(BF16), moe_w_up.astype(BF16), moe_w_down.astype(BF16)
    router_w_p = jnp.zeros((d, LANE), F32).at[:, :N_EXPERTS].set(router_w)

    x2 = x.reshape(t, d)
    for l in range(depth):
        mod6 = mod[l].reshape(batch * N_MOD, 1, d)
        proj = _in_projection(x2, norm1_g[l], mod6, w_in_p, l, seq)
        ya = _mixer_a(proj, conv_a_w[l], gn_a[l], batch, seq)
        yb = _mixer_b(proj, attn_sinks[l], gn_b[l], batch, seq)
        yc = _mixer_c(proj, ssm_conv_w[l], ssm_conv_b[l], dt_bias[l], a_log[l], d_skip[l], ssm_norm_g[l],
                      batch, seq)
        yd = _mixer_d(proj, batch, seq)
        x2, h2, logits_t = _out_projection(ya, yb, yc, yd, gn_d[l], w_out_b, l, x2, mod6, norm2_g[l],
                                           router_w_p, seq)
        idx, gates, cnt = _routing(logits_t, router_bias)
        blk_e, src_tok, dst_row = _dispatch_plan(idx, cnt, t, rb)
        y2 = _experts(blk_e, src_tok, dst_row, h2, wg_b, wu_b, wd_b, l, rb)
        x2 = _combine(x2, y2, gates[0:2].T, mod6, final_g, seq, final=(l == depth - 1))
    return x2.reshape(batch, seq, d)
```
